```python
import math
import jax, jax.numpy as jnp
from jax import lax
import numpy as np

D_MODEL = 1024
BATCH = 2
SEQ = 8192
DEPTH = 2

LRU_WIDTH = D_MODEL // 2
LRU_HEADS = 8
LRU_HEAD_DIM = LRU_WIDTH // LRU_HEADS
CONV_WIDTH = 4
LRU_C = 8.0
RWKV_WIDTH = D_MODEL // 2
RWKV_HEAD_DIM = 64
RWKV_HEADS = RWKV_WIDTH // RWKV_HEAD_DIM
DECAY_LORA = 64
AAA_LORA = 64
GATE_LORA = 128
RWKV_IN = 3 * RWKV_WIDTH + DECAY_LORA + AAA_LORA + GATE_LORA
EVEN_IN = 2 * LRU_WIDTH + RWKV_IN
GN_EPS = 64e-5
DIFF_HEAD_DIM = 64
DIFF_HEADS = D_MODEL // (2 * DIFF_HEAD_DIM)
DIFF_V_DIM = 2 * DIFF_HEAD_DIM
QKV_DIM = 2 * DIFF_HEADS * 2 * DIFF_HEAD_DIM + DIFF_HEADS * DIFF_V_DIM
ROPE_DIM = DIFF_HEAD_DIM // 4
ROPE_THETA = 500000.0
Q_BLOCK = 128
FFN_DIM = 2816
N_EXPERTS = 8
TOP_K = 2
EXPERT_DIM = 3584
RMS_EPS = 1e-6

kernel_name = 'hybrid_rglru_rwkv7_diffattn_moe'

F32 = jnp.float32


def rms_norm(x, g, eps=RMS_EPS):
    xf = x.astype(F32)
    y = xf * lax.rsqrt(jnp.mean(jnp.square(xf), axis=-1, keepdims=True) + eps)
    return (y * g.astype(F32)).astype(x.dtype)


def token_shift(x):
    return jnp.pad(x, ((0, 0), (1, 0), (0, 0)))[:, :-1]


def swiglu(x, wg, wu, wd):
    return (jax.nn.silu(x @ wg) * (x @ wu)) @ wd


def causal_dwconv(x, w, b):
    y = lax.conv_general_dilated(x, w[:, None, :].astype(x.dtype), window_strides=(1,),
                                 padding=[(CONV_WIDTH - 1, 0)],
                                 dimension_numbers=('NWC', 'WIO', 'NWC'),
                                 feature_group_count=x.shape[-1])
    return y + b


def rg_lru(x, wa, ba, wx, bx, lam):
    B, T, _ = x.shape
    xf = x.astype(F32)
    xh = xf.reshape(B, T, LRU_HEADS, LRU_HEAD_DIM)
    gate_a = jnp.einsum('bthi,hij->bthj', xh, wa.astype(F32)).reshape(B, T, LRU_WIDTH) + ba
    gate_x = jnp.einsum('bthi,hij->bthj', xh, wx.astype(F32)).reshape(B, T, LRU_WIDTH) + bx
    log_a = -LRU_C * jax.nn.sigmoid(gate_a) * jax.nn.softplus(-lam.astype(F32))
    a = jnp.exp(log_a)
    u = jnp.sqrt(-jnp.expm1(2.0 * log_a)) * jax.nn.sigmoid(gate_x) * xf

    def combine(l, r):
        return (l[0] * r[0], r[0] * l[1] + r[1])

    _, h = lax.associative_scan(combine, (a, u), axis=1)
    return h.astype(x.dtype)


def rwkv7_scan(r, w, k, v, kk, a):
    B, T, H, N = r.shape

    def step(S, inp):
        r_t, w_t, k_t, v_t, kk_t, a_t = inp
        sa = jnp.einsum('bhvk,bhk->bhv', S, -kk_t)
        S = S * w_t[:, :, None, :] + sa[..., None] * (kk_t * a_t)[:, :, None, :] \
            + v_t[..., None] * k_t[:, :, None, :]
        o = jnp.einsum('bhvk,bhk->bhv', S, r_t)
        return S, o

    xs = tuple(jnp.moveaxis(t, 1, 0) for t in (r, w, k, v, kk, a))
    S0 = jnp.zeros((B, H, N, N), F32)
    _, o = lax.scan(step, S0, xs)
    return jnp.moveaxis(o, 0, 1)


def rwkv7_mixer(p, mu, w0, w2, a0, a2, g2, k_k, k_a, r_k, gn_w, gn_b):
    B, T, _ = p.shape
    pf = p.astype(F32)
    ps = pf + (token_shift(pf) - pf) * mu
    idx = np.cumsum([RWKV_WIDTH, RWKV_WIDTH, RWKV_WIDTH, DECAY_LORA, AAA_LORA]).tolist()
    r, k, v, xw, xa, xg = jnp.split(ps, idx, axis=-1)
    wlog = -jax.nn.softplus(-(w0 + jnp.tanh(xw) @ w2)) - 0.5
    decay = jnp.exp(-jnp.exp(wlog))
    a = jax.nn.sigmoid(a0 + xa @ a2)
    g = jax.nn.sigmoid(xg) @ g2

    def hs(t):
        return t.reshape(B, T, RWKV_HEADS, RWKV_HEAD_DIM)

    kk = hs(k * k_k)
    kk = kk / jnp.maximum(jnp.linalg.norm(kk, axis=-1, keepdims=True), 1e-12)
    k = k * (1.0 + (a - 1.0) * k_a)
    rh, kh, vh = hs(r), hs(k), hs(v)
    o = rwkv7_scan(rh, hs(decay), kh, vh, kk, hs(a))
    mean = jnp.mean(o, axis=-1, keepdims=True)
    var = jnp.mean(jnp.square(o - mean), axis=-1, keepdims=True)
    on = ((o - mean) * lax.rsqrt(var + GN_EPS)).reshape(B, T, RWKV_WIDTH) * gn_w + gn_b
    bonus = (jnp.sum(rh * kh * r_k, axis=-1, keepdims=True) * vh).reshape(B, T, RWKV_WIDTH)
    return (on + bonus) * g


def even_mixer(xn, w_in, conv_w, conv_b, gate_a_w, gate_a_b, gate_x_w, gate_x_b, lru_lambda,
               shift_mu, w0, w2, a0, a2, g2, k_k, k_a, r_k, gn_w, gn_b, w_out):
    proj = xn @ w_in
    lru_x, lru_g, rw = jnp.split(proj, [LRU_WIDTH, 2 * LRU_WIDTH], axis=-1)
    h = rg_lru(causal_dwconv(lru_x, conv_w, conv_b), gate_a_w, gate_a_b, gate_x_w, gate_x_b, lru_lambda)
    y_lru = jax.nn.gelu(lru_g) * h
    y_rwkv = rwkv7_mixer(rw, shift_mu, w0, w2, a0, a2, g2, k_k, k_a, r_k, gn_w, gn_b)
    y = jnp.concatenate([y_lru, y_rwkv.astype(y_lru.dtype)], axis=-1)
    return y @ w_out


def partial_rope(x, positions):
    half = ROPE_DIM // 2
    inv_freq = ROPE_THETA ** (-jnp.arange(0, ROPE_DIM, 2, dtype=F32) / ROPE_DIM)
    ang = positions.astype(F32)[:, :, None] * inv_freq
    cos = jnp.cos(ang)[:, :, None, None, :]
    sin = jnp.sin(ang)[:, :, None, None, :]
    xf = x.astype(F32)
    x1 = xf[..., :half]
    x2 = xf[..., half:ROPE_DIM]
    out = jnp.concatenate([x1 * cos - x2 * sin, x2 * cos + x1 * sin, xf[..., ROPE_DIM:]], axis=-1)
    return out.astype(x.dtype)


def diff_attention(xn, positions, w_qkv, q_norm, k_norm, lq1, lk1, lq2, lk2, subln, w_o, lambda_init):
    B, T, _ = xn.shape
    qkv = xn @ w_qkv
    qd = DIFF_HEADS * 2 * DIFF_HEAD_DIM
    q, k, v = jnp.split(qkv, [qd, 2 * qd], axis=-1)
    q = q.reshape(B, T, DIFF_HEADS, 2, DIFF_HEAD_DIM)
    k = k.reshape(B, T, DIFF_HEADS, 2, DIFF_HEAD_DIM)
    v = v.reshape(B, T, DIFF_HEADS, DIFF_V_DIM).astype(F32)
    q = partial_rope(rms_norm(q, q_norm), positions)
    k = partial_rope(rms_norm(k, k_norm), positions)
    lam = jnp.exp(jnp.sum(lq1.astype(F32) * lk1.astype(F32))) \
        - jnp.exp(jnp.sum(lq2.astype(F32) * lk2.astype(F32))) + lambda_init
    n_blk = T // Q_BLOCK
    qb = jnp.moveaxis(q.reshape(B, n_blk, Q_BLOCK, DIFF_HEADS, 2, DIFF_HEAD_DIM), 1, 0)
    kpos = jnp.arange(T)
    scale = DIFF_HEAD_DIM ** -0.5

    def block(args):
        q_blk, bi = args
        s = jnp.einsum('bqhcd,bkhcd->bhcqk', q_blk, k, preferred_element_type=F32) * scale
        qpos = bi * Q_BLOCK + jnp.arange(Q_BLOCK)
        s = jnp.where(kpos[None, :] <= qpos[:, None], s, -jnp.inf)
        p = jax.nn.softmax(s, axis=-1)
        attn = p[:, :, 0] - lam * p[:, :, 1]
        return jnp.einsum('bhqk,bkhe->bqhe', attn, v)

    o = lax.map(block, (qb, jnp.arange(n_blk)))
    o = jnp.moveaxis(o, 0, 1).reshape(B, T, DIFF_HEADS, DIFF_V_DIM)
    o = rms_norm(o, subln) * (1.0 - lambda_init)
    return o.reshape(B, T, DIFF_HEADS * DIFF_V_DIM).astype(xn.dtype) @ w_o


def moe_swiglu(xn, router, wg, wu, wd):
    B, T, D = xn.shape
    xt = xn.reshape(B * T, D)
    logits = (xt @ router).astype(F32)
    topv, topi = lax.top_k(logits, TOP_K)
    gates = jax.nn.softmax(topv, axis=-1)
    combine = jnp.sum(jax.nn.one_hot(topi, N_EXPERTS, dtype=F32) * gates[..., None], axis=1)
    y = jnp.zeros((B * T, D), F32)
    for e in range(N_EXPERTS):
        y = y + combine[:, e:e + 1] * swiglu(xt, wg[e], wu[e], wd[e]).astype(F32)
    return y.reshape(B, T, D).astype(xn.dtype)


def setup_inputs(seed: int = 0) -> dict:
    key = jax.random.key(seed)
    keys = jax.random.split(key, 64)
    counter = [0]

    def nk():
        kk = keys[counter[0]]
        counter[0] += 1
        return kk

    def nrm(shape, scale):
        return jax.random.normal(nk(), shape, F32) * scale

    def gain(shape):
        return 1.0 + nrm(shape, 0.02)

    ne = (DEPTH + 1) // 2
    no = DEPTH // 2
    D = D_MODEL
    x = jax.random.normal(nk(), (BATCH, SEQ, D), F32)
    positions = jnp.broadcast_to(jnp.arange(SEQ, dtype=jnp.int32), (BATCH, SEQ))
    u = jax.random.uniform(nk(), (ne, LRU_WIDTH), F32, 0.9, 0.999)
    s = u ** (1.0 / LRU_C)
    return {
        'x': x,
        'positions': positions,
        'e_ln_mix': gain((ne, D)),
        'e_w_in': nrm((ne, D, EVEN_IN), D ** -0.5),
        'e_conv_w': nrm((ne, CONV_WIDTH, LRU_WIDTH), CONV_WIDTH ** -0.5),
        'e_conv_b': nrm((ne, LRU_WIDTH), 0.01),
        'e_gate_a_w': nrm((ne, LRU_HEADS, LRU_HEAD_DIM, LRU_HEAD_DIM), LRU_HEAD_DIM ** -0.5),
        'e_gate_a_b': nrm((ne, LRU_WIDTH), 0.01),
        'e_gate_x_w': nrm((ne, LRU_HEADS, LRU_HEAD_DIM, LRU_HEAD_DIM), LRU_HEAD_DIM ** -0.5),
        'e_gate_x_b': nrm((ne, LRU_WIDTH), 0.01),
        'e_lru_lambda': jnp.log(s) - jnp.log1p(-s),
        'e_shift_mu': jax.random.uniform(nk(), (ne, RWKV_IN), F32),
        'e_w0': jax.random.uniform(nk(), (ne, RWKV_WIDTH), F32, -6.0, 1.0),
        'e_w2': nrm((ne, DECAY_LORA, RWKV_WIDTH), 0.1),
        'e_a0': nrm((ne, RWKV_WIDTH), 0.1),
        'e_a2': nrm((ne, AAA_LORA, RWKV_WIDTH), 0.1),
        'e_g2': nrm((ne, GATE_LORA, RWKV_WIDTH), GATE_LORA ** -0.5),
        'e_k_k': 0.85 + nrm((ne, RWKV_WIDTH), 0.02),
        'e_k_a': gain((ne, RWKV_WIDTH)),
        'e_r_k': nrm((ne, RWKV_HEADS, RWKV_HEAD_DIM), 0.1),
        'e_gn_w': gain((ne, RWKV_WIDTH)),
        'e_gn_b': nrm((ne, RWKV_WIDTH), 0.01),
        'e_w_out': nrm((ne, D, D), D ** -0.5),
        'e_ln_ffn': gain((ne, D)),
        'e_ffn_gate': nrm((ne, D, FFN_DIM), D ** -0.5),
        'e_ffn_up': nrm((ne, D, FFN_DIM), D ** -0.5),
        'e_ffn_down': nrm((ne, FFN_DIM, D), FFN_DIM ** -0.5),
        'o_ln_mix': gain((no, D)),
        'o_w_qkv': nrm((no, D, QKV_DIM), D ** -0.5),
        'o_q_norm': gain((no, DIFF_HEAD_DIM)),
        'o_k_norm': gain((no, DIFF_HEAD_DIM)),
        'o_lambda_q1': nrm((no, DIFF_HEAD_DIM), 0.1),
        'o_lambda_k1': nrm((no, DIFF_HEAD_DIM), 0.1),
        'o_lambda_q2': nrm((no, DIFF_HEAD_DIM), 0.1),
        'o_lambda_k2': nrm((no, DIFF_HEAD_DIM), 0.1),
        'o_subln': gain((no, DIFF_V_DIM)),
        'o_w_o': nrm((no, DIFF_HEADS * DIFF_V_DIM, D), D ** -0.5),
        'o_ln_ffn': gain((no, D)),
        'o_router': nrm((no, D, N_EXPERTS), D ** -0.5),
        'o_moe_gate': nrm((no, N_EXPERTS, D, EXPERT_DIM), D ** -0.5),
        'o_moe_up': nrm((no, N_EXPERTS, D, EXPERT_DIM), D ** -0.5),
        'o_moe_down': nrm((no, N_EXPERTS, EXPERT_DIM, D), EXPERT_DIM ** -0.5),
    }


def reference(x, positions, e_ln_mix, e_w_in, e_conv_w, e_conv_b, e_gate_a_w, e_gate_a_b,
              e_gate_x_w, e_gate_x_b, e_lru_lambda, e_shift_mu, e_w0, e_w2, e_a0, e_a2, e_g2,
              e_k_k, e_k_a, e_r_k, e_gn_w, e_gn_b, e_w_out, e_ln_ffn, e_ffn_gate, e_ffn_up,
              e_ffn_down, o_ln_mix, o_w_qkv, o_q_norm, o_k_norm, o_lambda_q1, o_lambda_k1,
              o_lambda_q2, o_lambda_k2, o_subln, o_w_o, o_ln_ffn, o_router, o_moe_gate,
              o_moe_up, o_moe_down):
    for i in range(DEPTH):
        j = i // 2
        if i % 2 == 0:
            h = rms_norm(x, e_ln_mix[j])
            x = x + even_mixer(h, e_w_in[j], e_conv_w[j], e_conv_b[j], e_gate_a_w[j], e_gate_a_b[j],
                               e_gate_x_w[j], e_gate_x_b[j], e_lru_lambda[j], e_shift_mu[j], e_w0[j],
                               e_w2[j], e_a0[j], e_a2[j], e_g2[j], e_k_k[j], e_k_a[j], e_r_k[j],
                               e_gn_w[j], e_gn_b[j], e_w_out[j])
            h = rms_norm(x, e_ln_ffn[j])
            x = x + swiglu(h, e_ffn_gate[j], e_ffn_up[j], e_ffn_down[j])
        else:
            lambda_init = 0.8 - 0.6 * math.exp(-0.3 * i)
            h = rms_norm(x, o_ln_mix[j])
            x = x + diff_attention(h, positions, o_w_qkv[j], o_q_norm[j], o_k_norm[j],
                                   o_lambda_q1[j], o_lambda_k1[j], o_lambda_q2[j], o_lambda_k2[j],
                                   o_subln[j], o_w_o[j], lambda_init)
            h = rms_norm(x, o_ln_ffn[j])
            x = x + moe_swiglu(h, o_router[j], o_moe_gate[j], o_moe_up[j], o_moe_down[j])
    return x
```

```python
import functools
import math

import jax
import jax.numpy as jnp
from jax import lax
from jax.experimental import pallas as pl
from jax.experimental.pallas import tpu as pltpu

F32 = jnp.float32
BF16 = jnp.bfloat16
HIGHEST = lax.Precision.HIGHEST

LANES = 128
VMEM_LIMIT = 56 * 1024 * 1024

HEAD = 64
CONV_WIDTH = 4
LRU_C = 8.0
GN_EPS = 64e-5
RMS_EPS = 1e-6
ROPE_DIM = 16
ROPE_THETA = 500000.0
N_EXPERTS = 8
CHUNK = 64
NEG_BIG = -1e30


def _cparams(sem):
    return pltpu.CompilerParams(dimension_semantics=sem, vmem_limit_bytes=VMEM_LIMIT)


def _tile(n, pref):
    t = min(n, pref)
    assert n % t == 0, (n, pref)
    return t


def _nt(a, b, **kw):
    return lax.dot_general(a, b, (((1,), (1,)), ((), ())), preferred_element_type=F32, **kw)


def _tn(a, b, **kw):
    return lax.dot_general(a, b, (((0,), (0,)), ((), ())), preferred_element_type=F32, **kw)


def _dot(a, b, **kw):
    return jnp.dot(a, b, preferred_element_type=F32, **kw)


def _segsum(x, bd):
    hi = x.astype(BF16)
    lo = (x - hi.astype(F32)).astype(BF16)
    return _dot(hi, bd) + _dot(lo, bd)


def _rms(x, g):
    ms = jnp.mean(x * x, axis=-1, keepdims=True)
    return x * lax.rsqrt(ms + RMS_EPS) * g


def _sigmoid(x):
    return 1.0 / (1.0 + jnp.exp(-x))


def _softplus(x):
    return jnp.maximum(x, 0.0) + jnp.log1p(jnp.exp(-jnp.abs(x)))


def _norm_mm_kernel(x_ref, g_ref, w_ref, o_ref, h_ref):
    @pl.when(pl.program_id(1) == 0)
    def _():
        h_ref[...] = _rms(x_ref[...], g_ref[...]).astype(BF16)

    o_ref[...] = _dot(h_ref[...], w_ref[...]).astype(o_ref.dtype)


def norm_matmul(x, g, w, *, tm=1024, tn=512, out_dtype=F32):
    m, d = x.shape
    n = w.shape[1]
    tm, tn = _tile(m, tm), _tile(n, tn)
    return pl.pallas_call(
        _norm_mm_kernel,
        grid=(m // tm, n // tn),
        in_specs=[pl.BlockSpec((tm, d), lambda i, j: (i, 0)),
                  pl.BlockSpec((1, d), lambda i, j: (0, 0)),
                  pl.BlockSpec((d, tn), lambda i, j: (0, j))],
        out_specs=pl.BlockSpec((tm, tn), lambda i, j: (i, j)),
        out_shape=jax.ShapeDtypeStruct((m, n), out_dtype),
        scratch_shapes=[pltpu.VMEM((tm, d), BF16)],
        compiler_params=_cparams(("parallel", "arbitrary")),
    )(x, g.reshape(1, d), w)


def _mm_res_kernel(*refs, n_in):
    ys, ws = refs[:n_in], refs[n_in:2 * n_in]
    res_ref, o_ref = refs[2 * n_in], refs[2 * n_in + 1]
    acc = res_ref[...]
    for y_ref, w_ref in zip(ys, ws):
        acc = acc + _dot(y_ref[...], w_ref[...])
    o_ref[...] = acc


def matmul_residual(ys, ws, res, *, tm=1024, tn=512):
    m, n = res.shape
    tm, tn = _tile(m, tm), _tile(n, tn)
    n_in = len(ys)
    in_specs = [pl.BlockSpec((tm, y.shape[1]), lambda i, j: (i, 0)) for y in ys]
    in_specs += [pl.BlockSpec((w.shape[0], tn), lambda i, j: (0, j)) for w in ws]
    in_specs += [pl.BlockSpec((tm, tn), lambda i, j: (i, j))]
    return pl.pallas_call(
        functools.partial(_mm_res_kernel, n_in=n_in),
        grid=(m // tm, n // tn),
        in_specs=in_specs,
        out_specs=pl.BlockSpec((tm, tn), lambda i, j: (i, j)),
        out_shape=jax.ShapeDtypeStruct((m, n), F32),
        compiler_params=_cparams(("parallel", "arbitrary")),
    )(*ys, *ws, res)


def _lru_kernel(x_ref, gate_ref, cw_ref, cb_ref, wa_ref, ba_ref, wx_ref, bx_ref, lam_ref,
                o_ref, tail_ref, h_ref, *, tm):
    @pl.when(pl.program_id(1) == 0)
    def _():
        tail_ref[...] = jnp.zeros_like(tail_ref)
        h_ref[...] = jnp.zeros_like(h_ref)

    x = x_ref[...]
    xx = jnp.concatenate([tail_ref[...], x], axis=0)
    tail_ref[...] = x[tm - 8:, :]
    cw = cw_ref[...]
    xc = cb_ref[...] + cw[CONV_WIDTH - 1:CONV_WIDTH, :] * x
    for s in range(1, CONV_WIDTH):
        xc = xc + cw[CONV_WIDTH - 1 - s:CONV_WIDTH - s, :] * pltpu.roll(xx, s, axis=0)[8:, :]

    xb = xc.astype(BF16)
    gate_a = _dot(xb, wa_ref[...]) + ba_ref[...]
    gate_x = _dot(xb, wx_ref[...]) + bx_ref[...]
    log_a = -LRU_C * _sigmoid(gate_a) * _softplus(-lam_ref[...])
    a = jnp.exp(log_a)
    u = jnp.sqrt(1.0 - a * a) * _sigmoid(gate_x) * xc

    row = lax.broadcasted_iota(jnp.int32, a.shape, 0)
    s = 1
    while s < tm:
        keep = row >= s
        a_s = jnp.where(keep, pltpu.roll(a, s, axis=0), 1.0)
        u_s = jnp.where(keep, pltpu.roll(u, s, axis=0), 0.0)
        u = a * u_s + u
        a = a * a_s
        s *= 2
    h = a * h_ref[...] + u
    h_ref[...] = h[tm - 1:tm, :]
    o_ref[...] = (jax.nn.gelu(gate_ref[...]) * h).astype(o_ref.dtype)


def lru_branch(proj, batch, seq, conv_w, conv_b, wa_bd, ba, wx_bd, bx, lam, *, tm=256):
    m = proj.shape[0]
    w = lam.shape[0]
    tm = _tile(seq, tm)
    nt = seq // tm
    vec = lambda: pl.BlockSpec((1, w), lambda b, i: (0, 0))
    return pl.pallas_call(
        functools.partial(_lru_kernel, tm=tm),
        grid=(batch, nt),
        in_specs=[pl.BlockSpec((tm, w), lambda b, i: (b * nt + i, 0)),
                  pl.BlockSpec((tm, w), lambda b, i: (b * nt + i, 1)),
                  pl.BlockSpec((CONV_WIDTH, w), lambda b, i: (0, 0)),
                  vec(),
                  pl.BlockSpec((w, w), lambda b, i: (0, 0)), vec(),
                  pl.BlockSpec((w, w), lambda b, i: (0, 0)), vec(),
                  vec()],
        out_specs=pl.BlockSpec((tm, w), lambda b, i: (b * nt + i, 0)),
        out_shape=jax.ShapeDtypeStruct((m, w), BF16),
        scratch_shapes=[pltpu.VMEM((8, w), F32), pltpu.VMEM((1, w), F32)],
        compiler_params=_cparams(("parallel", "arbitrary")),
    )(proj, proj, conv_w, conv_b.reshape(1, w), wa_bd, ba.reshape(1, w), wx_bd, bx.reshape(1, w),
      lam.reshape(1, w))


def _rwkv_prep_kernel(pr_ref, pk_ref, pv_ref, pl_ref, qr_ref, qk_ref, qv_ref, ql_ref,
                      mur_ref, muk_ref, muv_ref, mul_ref, w0_ref, w2_ref, a0_ref, a2_ref, g2_ref,
                      kk_ref, ka_ref, bd_ref,
                      r_out, k_out, v_out, a_out, b_out, lw_out, g_out, *, tm, seq):
    first = (pl.program_id(0) * tm) % seq == 0

    def shift_mix(p_ref, q_ref, mu_ref):
        x = p_ref[...]
        prev = jnp.where(first, 0.0, q_ref[7:8, :])
        row = lax.broadcasted_iota(jnp.int32, x.shape, 0)
        xs = jnp.where(row == 0, prev, pltpu.roll(x, 1, axis=0))
        return x + (xs - x) * mu_ref[...]

    r = shift_mix(pr_ref, qr_ref, mur_ref)
    k = shift_mix(pk_ref, qk_ref, muk_ref)
    v = shift_mix(pv_ref, qv_ref, muv_ref)
    lo = shift_mix(pl_ref, ql_ref, mul_ref)

    wlog = -_softplus(-(w0_ref[...] + _dot(jnp.tanh(lo).astype(BF16), w2_ref[...]))) - 0.5
    a = _sigmoid(a0_ref[...] + _dot(lo.astype(BF16), a2_ref[...]))
    g = _dot(_sigmoid(lo).astype(BF16), g2_ref[...])

    kk = k * kk_ref[...]
    nrm = jnp.sqrt(_segsum(kk * kk, bd_ref[...]))
    kk = kk / jnp.maximum(nrm, 1e-12)

    r_out[...] = r
    k_out[...] = k * (1.0 + (a - 1.0) * ka_ref[...])
    v_out[...] = v
    a_out[...] = -kk
    b_out[...] = kk * a
    lw_out[...] = -jnp.exp(wlog)
    g_out[...] = g


def rwkv_prep(proj, seq, col0, mu, w0, w2p, a0, a2p, g2p, k_k, k_a, bd, *, tm=512):
    m = proj.shape[0]
    w = w0.shape[0]
    lw = w2p.shape[0]
    tm = _tile(seq, tm)
    cb = col0 // w
    lb = (col0 + 3 * w) // lw
    main = lambda c, width: pl.BlockSpec((tm, width), lambda i: (i, c))
    prev = lambda c, width: pl.BlockSpec((8, width), lambda i: (jnp.maximum(i * (tm // 8) - 1, 0), c))
    vec = lambda width: pl.BlockSpec((1, width), lambda i: (0, 0))
    mat = lambda a: pl.BlockSpec(a.shape, lambda i: (0, 0))
    mu_r, mu_k, mu_v, mu_l = (mu[None, 0:w], mu[None, w:2 * w], mu[None, 2 * w:3 * w], mu[None, 3 * w:])
    outs = [jax.ShapeDtypeStruct((m, w), F32)] * 7
    return pl.pallas_call(
        functools.partial(_rwkv_prep_kernel, tm=tm, seq=seq),
        grid=(m // tm,),
        in_specs=[main(cb, w), main(cb + 1, w), main(cb + 2, w), main(lb, lw),
                  prev(cb, w), prev(cb + 1, w), prev(cb + 2, w), prev(lb, lw),
                  vec(w), vec(w), vec(w), vec(lw),
                  vec(w), mat(w2p), vec(w), mat(a2p), mat(g2p), vec(w), vec(w), mat(bd)],
        out_specs=[pl.BlockSpec((tm, w), lambda i: (i, 0))] * 7,
        out_shape=outs,
        compiler_params=_cparams(("parallel",)),
    )(proj, proj, proj, proj, proj, proj, proj, proj, mu_r, mu_k, mu_v, mu_l,
      w0.reshape(1, w), w2p, a0.reshape(1, w), a2p, g2p, k_k.reshape(1, w), k_a.reshape(1, w), bd)


def _rwkv_scan_kernel(r_ref, k_ref, v_ref, a_ref, b_ref, lw_ref, g_ref, rk_ref, gnw_ref, gnb_ref,
                      bd_ref, o_ref, s_ref, *, width):
    @pl.when(pl.program_id(1) == 0)
    def _():
        s_ref[...] = jnp.zeros_like(s_ref)

    c = CHUNK
    lw = lw_ref[...]
    tri = (lax.broadcasted_iota(jnp.int32, (c, c), 0) >= lax.broadcasted_iota(jnp.int32, (c, c), 1))
    cum = _dot(tri.astype(F32), lw, precision=HIGHEST)
    tot = cum[c - 1:c, :]
    g_in = jnp.exp(cum)
    g_out = jnp.exp(-cum)
    g_suf = jnp.exp(tot - cum)
    g_tot = jnp.exp(tot)

    r, k, v = r_ref[...], k_ref[...], v_ref[...]
    rt = r * g_in
    kt = k * g_out
    at = a_ref[...] * jnp.exp(cum - lw)
    bt = b_ref[...] * g_out
    ks = k * g_suf
    bs = b_ref[...] * g_suf
    rkv = r * k * rk_ref[...]

    lane = lax.broadcasted_iota(jnp.int32, (c, LANES), 1)
    head0 = lane < HEAD
    i2 = lax.broadcasted_iota(jnp.int32, (2 * c, 2 * c), 0)
    j2 = lax.broadcasted_iota(jnp.int32, (2 * c, 2 * c), 1)
    strict = i2 > j2
    incl = i2 >= j2
    bd = bd_ref[...]

    def stack(x):
        return jnp.concatenate([jnp.where(head0, x, 0.0), jnp.where(head0, 0.0, x)], axis=0)

    for p in range(width // LANES):
        sl = slice(p * LANES, (p + 1) * LANES)
        a_s, r_s, k_s, b_s, v_s = stack(at[:, sl]), stack(rt[:, sl]), stack(kt[:, sl]), stack(bt[:, sl]), stack(v[:, sl])
        ar = jnp.concatenate([a_s, r_s], axis=0)
        kb = jnp.concatenate([k_s, b_s], axis=0)
        gram = _nt(ar, kb, precision=HIGHEST)
        state = s_ref[p]
        xs = _nt(ar, state, precision=HIGHEST)
        ak = jnp.where(strict, gram[:2 * c, :2 * c], 0.0)
        ab = jnp.where(strict, gram[:2 * c, 2 * c:], 0.0)
        rk = jnp.where(incl, gram[2 * c:, :2 * c], 0.0)
        rb = jnp.where(incl, gram[2 * c:, 2 * c:], 0.0)
        lv = _dot(jnp.concatenate([ak, rk], axis=0), v_s, precision=HIGHEST)
        u = xs[:2 * c] + lv[:2 * c]
        lpow = ab
        n_steps = int(math.log2(c))
        for i in range(n_steps):
            u = u + _dot(lpow, u, precision=HIGHEST)
            if i + 1 < n_steps:
                lpow = _dot(lpow, lpow, precision=HIGHEST)
        o_s = xs[2 * c:] + lv[2 * c:] + _dot(rb, u, precision=HIGHEST)
        o = o_s[:c] + o_s[c:]

        k_suf, b_suf = stack(ks[:, sl]), stack(bs[:, sl])
        upd = _tn(jnp.concatenate([v_s, u], axis=0), jnp.concatenate([k_suf, b_suf], axis=0),
                  precision=HIGHEST)
        s_ref[p] = state * g_tot[:, sl] + upd

        mean = _segsum(o, bd) * (1.0 / HEAD)
        d = o - mean
        var = _segsum(d * d, bd) * (1.0 / HEAD)
        on = d * lax.rsqrt(var + GN_EPS) * gnw_ref[:, sl] + gnb_ref[:, sl]
        bonus = _segsum(rkv[:, sl], bd) * v[:, sl]
        o_ref[:, sl] = ((on + bonus) * g_ref[:, sl]).astype(o_ref.dtype)


def rwkv_scan(r, k, v, a, b, lw, g, batch, seq, r_k, gn_w, gn_b, bd128):
    m, w = r.shape
    nc = seq // CHUNK
    blk = lambda: pl.BlockSpec((CHUNK, w), lambda bi, ci: (bi * nc + ci, 0))
    vec = lambda: pl.BlockSpec((1, w), lambda bi, ci: (0, 0))
    return pl.pallas_call(
        functools.partial(_rwkv_scan_kernel, width=w),
        grid=(batch, nc),
        in_specs=[blk() for _ in range(7)] + [vec(), vec(), vec(),
                                               pl.BlockSpec((LANES, LANES), lambda bi, ci: (0, 0))],
        out_specs=blk(),
        out_shape=jax.ShapeDtypeStruct((m, w), BF16),
        scratch_shapes=[pltpu.VMEM((w // LANES, LANES, LANES), F32)],
        compiler_params=_cparams(("parallel", "arbitrary")),
    )(r, k, v, a, b, lw, g, r_k.reshape(1, w), gn_w.reshape(1, w), gn_b.reshape(1, w), bd128)


def _ffn_kernel(x_ref, g_ref, wg_ref, wu_ref, wd_ref, o_ref, h_ref, acc_ref):
    f = pl.program_id(1)

    @pl.when(f == 0)
    def _():
        h_ref[...] = _rms(x_ref[...], g_ref[...]).astype(BF16)
        acc_ref[...] = x_ref[...]

    h = h_ref[...]
    gate = _dot(h, wg_ref[...])
    up = _dot(h, wu_ref[...])
    act = (gate * _sigmoid(gate) * up).astype(BF16)
    acc_ref[...] += _dot(act, wd_ref[...])

    @pl.when(f == pl.num_programs(1) - 1)
    def _():
        o_ref[...] = acc_ref[...]


def ffn(x, g, wg, wu, wd, *, tm=1024, tf=256):
    m, d = x.shape
    f = wg.shape[1]
    tm, tf = _tile(m, tm), _tile(f, tf)
    return pl.pallas_call(
        _ffn_kernel,
        grid=(m // tm, f // tf),
        in_specs=[pl.BlockSpec((tm, d), lambda i, j: (i, 0)),
                  pl.BlockSpec((1, d), lambda i, j: (0, 0)),
                  pl.BlockSpec((d, tf), lambda i, j: (0, j)),
                  pl.BlockSpec((d, tf), lambda i, j: (0, j)),
                  pl.BlockSpec((tf, d), lambda i, j: (j, 0))],
        out_specs=pl.BlockSpec((tm, d), lambda i, j: (i, 0)),
        out_shape=jax.ShapeDtypeStruct((m, d), F32),
        scratch_shapes=[pltpu.VMEM((tm, d), BF16), pltpu.VMEM((tm, d), F32)],
        compiler_params=_cparams(("parallel", "arbitrary")),
    )(x, g.reshape(1, d), wg, wu, wd)


def _qkv_kernel(x_ref, g_ref, w_ref, pos_ref, gain_ref, freq_ref, bd_ref, o_ref,
                h_ref, cos_ref, sina_ref, sinb_ref, *, n_rot_tiles, tn):
    j = pl.program_id(1)

    @pl.when(j == 0)
    def _():
        h_ref[...] = _rms(x_ref[...], g_ref[...]).astype(BF16)
        ang = pos_ref[...] * freq_ref[...]
        seg = lax.broadcasted_iota(jnp.int32, ang.shape, 1) % HEAD
        half = ROPE_DIM // 2
        cos_ref[...] = jnp.where(seg < ROPE_DIM, jnp.cos(ang), 1.0)
        sin = jnp.sin(ang)
        sina_ref[...] = jnp.where(seg < half, -sin, 0.0)
        sinb_ref[...] = jnp.where((seg >= half) & (seg < ROPE_DIM), sin, 0.0)

    y = _dot(h_ref[...], w_ref[...])

    @pl.when(j < n_rot_tiles)
    def _():
        half = ROPE_DIM // 2
        for c in range(tn // LANES):
            sl = slice(c * LANES, (c + 1) * LANES)
            yc = y[:, sl]
            ms = _segsum(yc * yc, bd_ref[...]) * (1.0 / HEAD)
            yn = yc * lax.rsqrt(ms + RMS_EPS) * gain_ref[:, sl]
            rot = (yn * cos_ref[...] + pltpu.roll(yn, LANES - half, axis=1) * sina_ref[...]
                   + pltpu.roll(yn, half, axis=1) * sinb_ref[...])
            o_ref[:, sl] = rot.astype(o_ref.dtype)

    @pl.when(j >= n_rot_tiles)
    def _():
        o_ref[...] = y.astype(o_ref.dtype)


def qkv_project(x, g, w, pos, gains, freq, bd128, n_rot_cols, *, tm=1024, tn=512):
    m, d = x.shape
    n = w.shape[1]
    tm, tn = _tile(m, tm), _tile(n, tn)
    assert n_rot_cols % tn == 0
    return pl.pallas_call(
        functools.partial(_qkv_kernel, n_rot_tiles=n_rot_cols // tn, tn=tn),
        grid=(m // tm, n // tn),
        in_specs=[pl.BlockSpec((tm, d), lambda i, j: (i, 0)),
                  pl.BlockSpec((1, d), lambda i, j: (0, 0)),
                  pl.BlockSpec((d, tn), lambda i, j: (0, j)),
                  pl.BlockSpec((tm, 1), lambda i, j: (i, 0)),
                  pl.BlockSpec((1, tn), lambda i, j: (0, j)),
                  pl.BlockSpec((1, LANES), lambda i, j: (0, 0)),
                  pl.BlockSpec((LANES, LANES), lambda i, j: (0, 0))],
        out_specs=pl.BlockSpec((tm, tn), lambda i, j: (i, j)),
        out_shape=jax.ShapeDtypeStruct((m, n), BF16),
        scratch_shapes=[pltpu.VMEM((tm, d), BF16), pltpu.VMEM((tm, LANES), F32),
                        pltpu.VMEM((tm, LANES), F32), pltpu.VMEM((tm, LANES), F32)],
        compiler_params=_cparams(("parallel", "arbitrary")),
    )(x, g.reshape(1, d), w, pos, gains, freq, bd128)


def _attn_kernel(q_ref, k_ref, v_ref, lam_ref, subln_ref, o_ref, m_ref, l_ref, acc_ref,
                 *, tq, lambda_init):
    qi = pl.program_id(2)
    q = q_ref[...]
    lane = lax.broadcasted_iota(jnp.int32, q.shape, 1)
    zero = jnp.zeros_like(q)
    qs = jnp.concatenate([jnp.where(lane < HEAD, q, zero), jnp.where(lane < HEAD, zero, q)], axis=0)

    m_ref[...] = jnp.full_like(m_ref, NEG_BIG)
    l_ref[...] = jnp.zeros_like(l_ref)
    acc_ref[...] = jnp.zeros_like(acc_ref)

    def step(j, masked):
        kb = k_ref[pl.ds(j * tq, tq), :]
        vb = v_ref[pl.ds(j * tq, tq), :]
        s = _nt(qs, kb)
        if masked:
            row = lax.broadcasted_iota(jnp.int32, (tq, tq), 0)
            col = lax.broadcasted_iota(jnp.int32, (tq, tq), 1)
            keep = jnp.concatenate([col <= row, col <= row], axis=0)
            s = jnp.where(keep, s, NEG_BIG)
        m_old = m_ref[...]
        m_new = jnp.maximum(m_old, jnp.max(s, axis=-1, keepdims=True))
        alpha = jnp.exp(m_old - m_new)
        p = jnp.exp(s - m_new)
        l_ref[...] = alpha * l_ref[...] + jnp.sum(p, axis=-1, keepdims=True)
        acc_ref[...] = alpha * acc_ref[...] + _dot(p.astype(BF16), vb)
        m_ref[...] = m_new

    def body(j, carry):
        step(j, False)
        return carry

    lax.fori_loop(0, qi, body, 0)
    step(qi, True)

    lq = lam_ref[...]
    lam = (jnp.exp(jnp.sum(lq[0:1] * lq[1:2], axis=-1, keepdims=True))
           - jnp.exp(jnp.sum(lq[2:3] * lq[3:4], axis=-1, keepdims=True)) + lambda_init)
    o = acc_ref[...] / l_ref[...]
    o = o[:tq] - lam * o[tq:]
    o = _rms(o, subln_ref[...]) * (1.0 - lambda_init)
    o_ref[...] = o.astype(o_ref.dtype)


def diff_attention(qkv, batch, seq, n_heads, lam_params, subln, lambda_init, *, tq=256):
    m = qkv.shape[0]
    tq = _tile(seq, tq)
    nq = seq // tq
    return pl.pallas_call(
        functools.partial(_attn_kernel, tq=tq, lambda_init=lambda_init),
        grid=(batch, n_heads, nq),
        in_specs=[pl.BlockSpec((tq, LANES), lambda b, h, i: (b * nq + i, h)),
                  pl.BlockSpec((seq, LANES), lambda b, h, i: (b, n_heads + h)),
                  pl.BlockSpec((seq, LANES), lambda b, h, i: (b, 2 * n_heads + h)),
                  pl.BlockSpec((4, HEAD), lambda b, h, i: (0, 0)),
                  pl.BlockSpec((1, LANES), lambda b, h, i: (0, 0))],
        out_specs=pl.BlockSpec((tq, LANES), lambda b, h, i: (b * nq + i, h)),
        out_shape=jax.ShapeDtypeStruct((m, n_heads * LANES), BF16),
        scratch_shapes=[pltpu.VMEM((2 * tq, 1), F32), pltpu.VMEM((2 * tq, 1), F32),
                        pltpu.VMEM((2 * tq, LANES), F32)],
        compiler_params=_cparams(("parallel", "parallel", "arbitrary")),
    )(qkv, qkv, qkv, lam_params, subln.reshape(1, LANES))


def _moe_kernel(x_ref, g_ref, router_ref, wg_ref, wu_ref, wd_ref, o_ref, h_ref, comb_ref, acc_ref):
    e = pl.program_id(1)
    f = pl.program_id(2)

    @pl.when((e == 0) & (f == 0))
    def _():
        x = x_ref[...]
        h = _rms(x, g_ref[...])
        h_ref[...] = h.astype(BF16)
        acc_ref[...] = x
        logits = _dot(h, router_ref[...], precision=HIGHEST)
        lane = lax.broadcasted_iota(jnp.int32, logits.shape, 1)
        logits = jnp.where(lane < N_EXPERTS, logits, NEG_BIG)
        v1 = jnp.max(logits, axis=-1, keepdims=True)
        i1 = jnp.min(jnp.where(logits == v1, lane, LANES), axis=-1, keepdims=True)
        rest = jnp.where(lane == i1, NEG_BIG, logits)
        v2 = jnp.max(rest, axis=-1, keepdims=True)
        i2 = jnp.min(jnp.where(rest == v2, lane, LANES), axis=-1, keepdims=True)
        e2 = jnp.exp(v2 - v1)
        g1 = 1.0 / (1.0 + e2)
        g2 = e2 / (1.0 + e2)
        comb_ref[...] = jnp.where(lane == i1, g1, 0.0) + jnp.where(lane == i2, g2, 0.0)

    comb = comb_ref[...]
    lane = lax.broadcasted_iota(jnp.int32, comb.shape, 1)
    ce = jnp.sum(jnp.where(lane == e, comb, 0.0), axis=-1, keepdims=True)

    h = h_ref[...]
    gate = _dot(h, wg_ref[...])
    up = _dot(h, wu_ref[...])
    act = (gate * _sigmoid(gate) * up * ce).astype(BF16)
    acc_ref[...] += _dot(act, wd_ref[...])

    @pl.when((e == pl.num_programs(1) - 1) & (f == pl.num_programs(2) - 1))
    def _():
        o_ref[...] = acc_ref[...]


def moe(x, g, router_p, wg, wu, wd, *, tm=1024, tf=512):
    m, d = x.shape
    ne, _, f = wg.shape
    tm, tf = _tile(m, tm), _tile(f, tf)
    return pl.pallas_call(
        _moe_kernel,
        grid=(m // tm, ne, f // tf),
        in_specs=[pl.BlockSpec((tm, d), lambda i, e, j: (i, 0)),
                  pl.BlockSpec((1, d), lambda i, e, j: (0, 0)),
                  pl.BlockSpec((d, LANES), lambda i, e, j: (0, 0)),
                  pl.BlockSpec((None, d, tf), lambda i, e, j: (e, 0, j)),
                  pl.BlockSpec((None, d, tf), lambda i, e, j: (e, 0, j)),
                  pl.BlockSpec((None, tf, d), lambda i, e, j: (e, j, 0))],
        out_specs=pl.BlockSpec((tm, d), lambda i, e, j: (i, 0)),
        out_shape=jax.ShapeDtypeStruct((m, d), F32),
        scratch_shapes=[pltpu.VMEM((tm, d), BF16), pltpu.VMEM((tm, LANES), F32),
                        pltpu.VMEM((tm, d), F32)],
        compiler_params=_cparams(("parallel", "arbitrary", "arbitrary")),
    )(x, g.reshape(1, d), router_p, wg, wu, wd)


def _block_diag(blocks):
    n, h, _ = blocks.shape
    eye = jnp.eye(n, dtype=blocks.dtype)
    return (eye[:, None, :, None] * blocks[:, :, None, :]).reshape(n * h, n * h)


def _seg_ones(n):
    seg = jnp.arange(n) // HEAD
    return (seg[:, None] == seg[None, :]).astype(BF16)


def _even_layer(x, batch, seq, ln_mix, w_in, conv_w, conv_b, gate_a_w, gate_a_b, gate_x_w, gate_x_b,
                lru_lambda, shift_mu, w0, w2, a0, a2, g2, k_k, k_a, r_k, gn_w, gn_b, w_out,
                ln_ffn, ffn_gate, ffn_up, ffn_down):
    lru_w = lru_lambda.shape[0]
    rw_w = w0.shape[0]
    dl, al, gl = w2.shape[0], a2.shape[0], g2.shape[0]
    n_in = w_in.shape[1]
    proj = norm_matmul(x, ln_mix, w_in.astype(BF16), tn=n_in // 2)

    y_lru = lru_branch(proj, batch, seq, conv_w, conv_b,
                       _block_diag(gate_a_w).astype(BF16), gate_a_b,
                       _block_diag(gate_x_w).astype(BF16), gate_x_b, lru_lambda)

    lora = dl + al + gl
    zeros = lambda n: jnp.zeros((n, rw_w), F32)
    w2p = jnp.concatenate([w2, zeros(al + gl)], axis=0).astype(BF16)
    a2p = jnp.concatenate([zeros(dl), a2, zeros(gl)], axis=0).astype(BF16)
    g2p = jnp.concatenate([zeros(dl + al), g2], axis=0).astype(BF16)
    assert lora == w2p.shape[0]
    r, k, v, a, b, lw, g = rwkv_prep(proj, seq, 2 * lru_w, shift_mu, w0, w2p, a0, a2p, g2p, k_k, k_a,
                                     _seg_ones(rw_w))
    y_rwkv = rwkv_scan(r, k, v, a, b, lw, g, batch, seq, r_k.reshape(-1), gn_w, gn_b, _seg_ones(LANES))

    w_out = w_out.astype(BF16)
    x = matmul_residual([y_lru, y_rwkv], [w_out[:lru_w], w_out[lru_w:]], x)
    return ffn(x, ln_ffn, ffn_gate.astype(BF16), ffn_up.astype(BF16), ffn_down.astype(BF16))


def _odd_layer(x, pos, batch, seq, layer_idx, ln_mix, w_qkv, q_norm, k_norm, lq1, lk1, lq2, lk2, subln,
               w_o, ln_ffn, router, moe_gate, moe_up, moe_down):
    d = x.shape[1]
    n_heads = d // (2 * HEAD)
    qd = n_heads * 2 * HEAD
    lambda_init = 0.8 - 0.6 * math.exp(-0.3 * layer_idx)
    reps = qd // HEAD
    gains = jnp.concatenate([jnp.tile(q_norm, reps) * (HEAD ** -0.5),
                             jnp.tile(k_norm, reps),
                             jnp.ones((w_qkv.shape[1] - 2 * qd,), F32)])[None, :]
    seg = jnp.arange(LANES) % HEAD
    inv_freq = ROPE_THETA ** (-(2.0 * (seg % (ROPE_DIM // 2))).astype(F32) / ROPE_DIM)
    freq = jnp.where(seg < ROPE_DIM, inv_freq, 0.0)[None, :].astype(F32)
    qkv = qkv_project(x, ln_mix, w_qkv.astype(BF16), pos, gains, freq, _seg_ones(LANES), 2 * qd)
    lam_params = jnp.stack([lq1, lk1, lq2, lk2]).astype(F32)
    o = diff_attention(qkv, batch, seq, n_heads, lam_params, subln, lambda_init)
    x = matmul_residual([o], [w_o.astype(BF16)], x)
    router_p = jnp.pad(router, ((0, 0), (0, LANES - router.shape[1])))
    return moe(x, ln_ffn, router_p, moe_gate.astype(BF16), moe_up.astype(BF16), moe_down.astype(BF16))


def kernel(x, positions, e_ln_mix, e_w_in, e_conv_w, e_conv_b, e_gate_a_w, e_gate_a_b, e_gate_x_w, e_gate_x_b, e_lru_lambda, e_shift_mu, e_w0, e_w2, e_a0, e_a2, e_g2, e_k_k, e_k_a, e_r_k, e_gn_w, e_gn_b, e_w_out, e_ln_ffn, e_ffn_gate, e_ffn_up, e_ffn_down, o_ln_mix, o_w_qkv, o_q_norm, o_k_norm, o_lambda_q1, o_lambda_k1, o_lambda_q2, o_lambda_k2, o_subln, o_w_o, o_ln_ffn, o_router, o_moe_gate, o_moe_up, o_moe_down):
    batch, seq, d = x.shape
    depth = e_ln_mix.shape[0] + o_ln_mix.shape[0]
    xf = x.reshape(batch * seq, d)
    pos = positions.reshape(batch * seq, 1).astype(F32)
    for i in range(depth):
        j = i // 2
        if i % 2 == 0:
            xf = _even_layer(xf, batch, seq, e_ln_mix[j], e_w_in[j], e_conv_w[j], e_conv_b[j],
                             e_gate_a_w[j], e_gate_a_b[j], e_gate_x_w[j], e_gate_x_b[j],
                             e_lru_lambda[j], e_shift_mu[j], e_w0[j], e_w2[j], e_a0[j], e_a2[j],
                             e_g2[j], e_k_k[j], e_k_a[j], e_r_k[j], e_gn_w[j], e_gn_b[j], e_w_out[j],
                             e_ln_ffn[j], e_ffn_gate[j], e_ffn_up[j], e_ffn_down[j])
        else:
            xf = _odd_layer(xf, pos, batch, seq, i, o_ln_mix[j], o_w_qkv[j], o_q_norm[j], o_k_norm[j],
                            o_lambda_q1[j], o_lambda_k1[j], o_lambda_q2[j], o_lambda_k2[j], o_subln[j],
                            o_w_o[j], o_ln_ffn[j], o_router[j], o_moe_gate[j], o_moe_up[j],
                            o_moe_down[j])
    return xf.reshape(batch, seq, d)
```

```python
import functools
import math

import jax
import jax.numpy as jnp
from jax import lax
from jax.experimental import pallas as pl
from jax.experimental.pallas import tpu as pltpu

F32 = jnp.float32
BF16 = jnp.bfloat16
HIGHEST = lax.Precision.HIGHEST

LANES = 128
VMEM_LIMIT = 56 * 1024 * 1024

HEAD = 64
CONV_WIDTH = 4
LRU_C = 8.0
GN_EPS = 64e-5
RMS_EPS = 1e-6
ROPE_DIM = 16
ROPE_THETA = 500000.0
N_EXPERTS = 8
CHUNK = 64
NEG_BIG = -1e30
LOG2E = 1.4426950408889634
MAX_EXP2_ARG = 60.0


def _cparams(sem):
    return pltpu.CompilerParams(dimension_semantics=sem, vmem_limit_bytes=VMEM_LIMIT)


def _tile(n, pref):
    t = min(n, pref)
    assert n % t == 0, (n, pref)
    return t


def _nt(a, b, **kw):
    return lax.dot_general(a, b, (((1,), (1,)), ((), ())), preferred_element_type=F32, **kw)


def _tn(a, b, **kw):
    return lax.dot_general(a, b, (((0,), (0,)), ((), ())), preferred_element_type=F32, **kw)


def _dot(a, b, **kw):
    return jnp.dot(a, b, preferred_element_type=F32, **kw)


def _segsum(x, bd):
    hi = x.astype(BF16)
    lo = (x - hi.astype(F32)).astype(BF16)
    return _dot(hi, bd) + _dot(lo, bd)


def _rms(x, g):
    ms = jnp.mean(x * x, axis=-1, keepdims=True)
    return x * lax.rsqrt(ms + RMS_EPS) * g


def _sigmoid(x):
    return 1.0 / (1.0 + jnp.exp(-x))


def _softplus(x):
    return jnp.maximum(x, 0.0) + jnp.log1p(jnp.exp(-jnp.abs(x)))


def _norm_mm_kernel(x_ref, g_ref, w_ref, o_ref, h_ref):
    @pl.when(pl.program_id(1) == 0)
    def _():
        h_ref[...] = _rms(x_ref[...], g_ref[...]).astype(BF16)

    o_ref[...] = _dot(h_ref[...], w_ref[...]).astype(o_ref.dtype)


def norm_matmul(x, g, w, *, tm=1024, tn=512, out_dtype=F32):
    m, d = x.shape
    n = w.shape[1]
    tm, tn = _tile(m, tm), _tile(n, tn)
    return pl.pallas_call(
        _norm_mm_kernel,
        grid=(m // tm, n // tn),
        in_specs=[pl.BlockSpec((tm, d), lambda i, j: (i, 0)),
                  pl.BlockSpec((1, d), lambda i, j: (0, 0)),
                  pl.BlockSpec((d, tn), lambda i, j: (0, j))],
        out_specs=pl.BlockSpec((tm, tn), lambda i, j: (i, j)),
        out_shape=jax.ShapeDtypeStruct((m, n), out_dtype),
        scratch_shapes=[pltpu.VMEM((tm, d), BF16)],
        compiler_params=_cparams(("parallel", "arbitrary")),
    )(x, g.reshape(1, d), w)


def _mm_res_kernel(*refs, n_in):
    ys, ws = refs[:n_in], refs[n_in:2 * n_in]
    res_ref, o_ref = refs[2 * n_in], refs[2 * n_in + 1]
    acc = res_ref[...]
    for y_ref, w_ref in zip(ys, ws):
        acc = acc + _dot(y_ref[...], w_ref[...])
    o_ref[...] = acc


def matmul_residual(ys, ws, res, *, tm=1024, tn=512):
    m, n = res.shape
    tm, tn = _tile(m, tm), _tile(n, tn)
    n_in = len(ys)
    in_specs = [pl.BlockSpec((tm, y.shape[1]), lambda i, j: (i, 0)) for y in ys]
    in_specs += [pl.BlockSpec((w.shape[0], tn), lambda i, j: (0, j)) for w in ws]
    in_specs += [pl.BlockSpec((tm, tn), lambda i, j: (i, j))]
    return pl.pallas_call(
        functools.partial(_mm_res_kernel, n_in=n_in),
        grid=(m // tm, n // tn),
        in_specs=in_specs,
        out_specs=pl.BlockSpec((tm, tn), lambda i, j: (i, j)),
        out_shape=jax.ShapeDtypeStruct((m, n), F32),
        compiler_params=_cparams(("parallel", "arbitrary")),
    )(*ys, *ws, res)


def _lru_kernel(x_ref, gate_ref, cw_ref, cb_ref, wa_ref, ba_ref, wx_ref, bx_ref, lam_ref,
                o_ref, tail_ref, h_ref, *, tm):
    @pl.when(pl.program_id(1) == 0)
    def _():
        tail_ref[...] = jnp.zeros_like(tail_ref)
        h_ref[...] = jnp.zeros_like(h_ref)

    x = x_ref[...]
    xx = jnp.concatenate([tail_ref[...], x], axis=0)
    tail_ref[...] = x[tm - 8:, :]
    cw = cw_ref[...]
    xc = cb_ref[...] + cw[CONV_WIDTH - 1:CONV_WIDTH, :] * x
    for s in range(1, CONV_WIDTH):
        xc = xc + cw[CONV_WIDTH - 1 - s:CONV_WIDTH - s, :] * pltpu.roll(xx, s, axis=0)[8:, :]

    xb = xc.astype(BF16)
    gate_a = _dot(xb, wa_ref[...]) + ba_ref[...]
    gate_x = _dot(xb, wx_ref[...]) + bx_ref[...]
    log_a = -LRU_C * _sigmoid(gate_a) * _softplus(-lam_ref[...])
    a = jnp.exp(log_a)
    u = jnp.sqrt(1.0 - a * a) * _sigmoid(gate_x) * xc

    row = lax.broadcasted_iota(jnp.int32, a.shape, 0)
    s = 1
    while s < tm:
        keep = row >= s
        a_s = jnp.where(keep, pltpu.roll(a, s, axis=0), 1.0)
        u_s = jnp.where(keep, pltpu.roll(u, s, axis=0), 0.0)
        u = a * u_s + u
        a = a * a_s
        s *= 2
    h = a * h_ref[...] + u
    h_ref[...] = h[tm - 1:tm, :]
    o_ref[...] = (jax.nn.gelu(gate_ref[...]) * h).astype(o_ref.dtype)


def lru_branch(proj, batch, seq, conv_w, conv_b, wa_bd, ba, wx_bd, bx, lam, *, tm=256):
    m = proj.shape[0]
    w = lam.shape[0]
    tm = _tile(seq, tm)
    nt = seq // tm
    vec = lambda: pl.BlockSpec((1, w), lambda b, i: (0, 0))
    return pl.pallas_call(
        functools.partial(_lru_kernel, tm=tm),
        grid=(batch, nt),
        in_specs=[pl.BlockSpec((tm, w), lambda b, i: (b * nt + i, 0)),
                  pl.BlockSpec((tm, w), lambda b, i: (b * nt + i, 1)),
                  pl.BlockSpec((CONV_WIDTH, w), lambda b, i: (0, 0)),
                  vec(),
                  pl.BlockSpec((w, w), lambda b, i: (0, 0)), vec(),
                  pl.BlockSpec((w, w), lambda b, i: (0, 0)), vec(),
                  vec()],
        out_specs=pl.BlockSpec((tm, w), lambda b, i: (b * nt + i, 0)),
        out_shape=jax.ShapeDtypeStruct((m, w), BF16),
        scratch_shapes=[pltpu.VMEM((8, w), F32), pltpu.VMEM((1, w), F32)],
        compiler_params=_cparams(("parallel", "arbitrary")),
    )(proj, proj, conv_w, conv_b.reshape(1, w), wa_bd, ba.reshape(1, w), wx_bd, bx.reshape(1, w),
      lam.reshape(1, w))


def _rwkv_prep_kernel(pr_ref, pk_ref, pv_ref, pl_ref, qr_ref, qk_ref, qv_ref, ql_ref,
                      mur_ref, muk_ref, muv_ref, mul_ref, w0_ref, w2_ref, a0_ref, a2_ref, g2_ref,
                      kk_ref, ka_ref, bd_ref,
                      r_out, k_out, v_out, a_out, b_out, lw_out, g_out, *, tm, seq):
    first = (pl.program_id(0) * tm) % seq == 0

    def shift_mix(p_ref, q_ref, mu_ref):
        x = p_ref[...]
        prev = jnp.where(first, 0.0, q_ref[7:8, :])
        row = lax.broadcasted_iota(jnp.int32, x.shape, 0)
        xs = jnp.where(row == 0, prev, pltpu.roll(x, 1, axis=0))
        return x + (xs - x) * mu_ref[...]

    r = shift_mix(pr_ref, qr_ref, mur_ref)
    k = shift_mix(pk_ref, qk_ref, muk_ref)
    v = shift_mix(pv_ref, qv_ref, muv_ref)
    lo = shift_mix(pl_ref, ql_ref, mul_ref)

    wlog = -_softplus(-(w0_ref[...] + _dot(jnp.tanh(lo).astype(BF16), w2_ref[...]))) - 0.5
    a = _sigmoid(a0_ref[...] + _dot(lo.astype(BF16), a2_ref[...]))
    g = _dot(_sigmoid(lo).astype(BF16), g2_ref[...])

    kk = k * kk_ref[...]
    nrm = jnp.sqrt(_segsum(kk * kk, bd_ref[...]))
    kk = kk / jnp.maximum(nrm, 1e-12)

    r_out[...] = r
    k_out[...] = k * (1.0 + (a - 1.0) * ka_ref[...])
    v_out[...] = v
    a_out[...] = -kk
    b_out[...] = kk * a
    lw_out[...] = -jnp.exp(wlog)
    g_out[...] = g


def rwkv_prep(proj, seq, col0, mu, w0, w2p, a0, a2p, g2p, k_k, k_a, bd, *, tm=512):
    m = proj.shape[0]
    w = w0.shape[0]
    lw = w2p.shape[0]
    tm = _tile(seq, tm)
    cb = col0 // w
    lb = (col0 + 3 * w) // lw
    main = lambda c, width: pl.BlockSpec((tm, width), lambda i: (i, c))
    prev = lambda c, width: pl.BlockSpec((8, width), lambda i: (jnp.maximum(i * (tm // 8) - 1, 0), c))
    vec = lambda width: pl.BlockSpec((1, width), lambda i: (0, 0))
    mat = lambda a: pl.BlockSpec(a.shape, lambda i: (0, 0))
    mu_r, mu_k, mu_v, mu_l = (mu[None, 0:w], mu[None, w:2 * w], mu[None, 2 * w:3 * w], mu[None, 3 * w:])
    outs = [jax.ShapeDtypeStruct((m, w), F32)] * 7
    return pl.pallas_call(
        functools.partial(_rwkv_prep_kernel, tm=tm, seq=seq),
        grid=(m // tm,),
        in_specs=[main(cb, w), main(cb + 1, w), main(cb + 2, w), main(lb, lw),
                  prev(cb, w), prev(cb + 1, w), prev(cb + 2, w), prev(lb, lw),
                  vec(w), vec(w), vec(w), vec(lw),
                  vec(w), mat(w2p), vec(w), mat(a2p), mat(g2p), vec(w), vec(w), mat(bd)],
        out_specs=[pl.BlockSpec((tm, w), lambda i: (i, 0))] * 7,
        out_shape=outs,
        compiler_params=_cparams(("parallel",)),
    )(proj, proj, proj, proj, proj, proj, proj, proj, mu_r, mu_k, mu_v, mu_l,
      w0.reshape(1, w), w2p, a0.reshape(1, w), a2p, g2p, k_k.reshape(1, w), k_a.reshape(1, w), bd)


def _rwkv_scan_kernel(r_ref, k_ref, v_ref, a_ref, b_ref, lw_ref, g_ref, rk_ref, gnw_ref, gnb_ref,
                      bd_ref, o_ref, s_ref, *, batch, width):
    @pl.when(pl.program_id(0) == 0)
    def _():
        s_ref[...] = jnp.zeros_like(s_ref)

    c = CHUNK
    tri = (lax.broadcasted_iota(jnp.int32, (c, c), 0) >= lax.broadcasted_iota(jnp.int32, (c, c), 1))
    lane = lax.broadcasted_iota(jnp.int32, (c, LANES), 1)
    head0 = lane < HEAD
    i2 = lax.broadcasted_iota(jnp.int32, (2 * c, 2 * c), 0)
    j2 = lax.broadcasted_iota(jnp.int32, (2 * c, 2 * c), 1)
    strict = i2 > j2
    incl = i2 >= j2
    bd = bd_ref[...]
    n_steps = int(math.log2(c))

    def stack(x):
        xb = x.astype(BF16)
        zero = jnp.zeros_like(xb)
        return jnp.concatenate([jnp.where(head0, xb, zero), jnp.where(head0, zero, xb)], axis=0)

    chains = []
    for bi in range(batch):
        lw = lw_ref[bi]
        cum = _dot(tri.astype(F32), lw, precision=HIGHEST)
        tot = cum[c - 1:c, :]
        g_out = jnp.exp(-cum)
        g_suf = jnp.exp(tot - cum)
        g_tot = jnp.exp(tot)
        r, k, v, b = r_ref[bi], k_ref[bi], v_ref[bi], b_ref[bi]
        rt = r * jnp.exp(cum)
        kt = k * g_out
        at = a_ref[bi] * jnp.exp(cum - lw)
        bt = b * g_out
        ks = k * g_suf
        bs = b * g_suf
        rkv = r * k * rk_ref[...]
        for p in range(width // LANES):
            sl = slice(p * LANES, (p + 1) * LANES)
            chains.append(dict(
                bi=bi, p=p, sl=sl, v=v[:, sl], rkv=rkv[:, sl], g_tot=g_tot[:, sl],
                v_s=stack(v[:, sl]),
                ar=jnp.concatenate([stack(at[:, sl]), stack(rt[:, sl])], axis=0),
                kb=jnp.concatenate([stack(kt[:, sl]), stack(bt[:, sl])], axis=0),
                suf=jnp.concatenate([stack(ks[:, sl]), stack(bs[:, sl])], axis=0)))

    for ch in chains:
        ch["state"] = s_ref[ch["bi"], ch["p"]]
        ch["gram"] = _nt(ch["ar"], ch["kb"])
        ch["xs"] = _nt(ch["ar"], ch["state"].astype(BF16))
    for ch in chains:
        gram = ch["gram"]
        ak = jnp.where(strict, gram[:2 * c, :2 * c], 0.0)
        rk = jnp.where(incl, gram[2 * c:, :2 * c], 0.0)
        ch["lpow"] = jnp.where(strict, gram[:2 * c, 2 * c:], 0.0)
        ch["rb"] = jnp.where(incl, gram[2 * c:, 2 * c:], 0.0).astype(BF16)
        ch["lv"] = _dot(jnp.concatenate([ak, rk], axis=0).astype(BF16), ch["v_s"])
    for ch in chains:
        ch["u"] = ch["xs"][:2 * c] + ch["lv"][:2 * c]
    for i in range(n_steps):
        for ch in chains:
            lb = ch["lpow"].astype(BF16)
            ch["u"] = ch["u"] + _dot(lb, ch["u"].astype(BF16))
            if i + 1 < n_steps:
                ch["lpow"] = _dot(lb, lb)
    for ch in chains:
        ub = ch["u"].astype(BF16)
        o_s = ch["xs"][2 * c:] + ch["lv"][2 * c:] + _dot(ch["rb"], ub)
        ch["o"] = o_s[:c] + o_s[c:]
        upd = _tn(jnp.concatenate([ch["v_s"], ub], axis=0), ch["suf"])
        s_ref[ch["bi"], ch["p"]] = ch["state"] * ch["g_tot"] + upd
    for ch in chains:
        bi, sl, o = ch["bi"], ch["sl"], ch["o"]
        mean = _segsum(o, bd) * (1.0 / HEAD)
        d = o - mean
        var = _segsum(d * d, bd) * (1.0 / HEAD)
        on = d * lax.rsqrt(var + GN_EPS) * gnw_ref[:, sl] + gnb_ref[:, sl]
        bonus = _segsum(ch["rkv"], bd) * ch["v"]
        o_ref[bi, :, sl] = ((on + bonus) * g_ref[bi, :, sl]).astype(o_ref.dtype)


def rwkv_scan(r, k, v, a, b, lw, g, batch, seq, r_k, gn_w, gn_b, bd128):
    m, w = r.shape
    nc = seq // CHUNK
    blk = lambda: pl.BlockSpec((batch, CHUNK, w), lambda ci: (0, ci, 0))
    vec = lambda: pl.BlockSpec((1, w), lambda ci: (0, 0))
    as3d = lambda x: x.reshape(batch, seq, w)
    out = pl.pallas_call(
        functools.partial(_rwkv_scan_kernel, batch=batch, width=w),
        grid=(nc,),
        in_specs=[blk() for _ in range(7)] + [vec(), vec(), vec(),
                                               pl.BlockSpec((LANES, LANES), lambda ci: (0, 0))],
        out_specs=blk(),
        out_shape=jax.ShapeDtypeStruct((batch, seq, w), BF16),
        scratch_shapes=[pltpu.VMEM((batch, w // LANES, LANES, LANES), F32)],
        compiler_params=_cparams(("arbitrary",)),
    )(as3d(r), as3d(k), as3d(v), as3d(a), as3d(b), as3d(lw), as3d(g),
      r_k.reshape(1, w), gn_w.reshape(1, w), gn_b.reshape(1, w), bd128)
    return out.reshape(m, w)


def _ffn_kernel(x_ref, g_ref, wg_ref, wu_ref, wd_ref, o_ref, h_ref, acc_ref):
    f = pl.program_id(1)

    @pl.when(f == 0)
    def _():
        h_ref[...] = _rms(x_ref[...], g_ref[...]).astype(BF16)
        acc_ref[...] = x_ref[...]

    h = h_ref[...]
    gate = _dot(h, wg_ref[...])
    up = _dot(h, wu_ref[...])
    act = (gate * _sigmoid(gate) * up).astype(BF16)
    acc_ref[...] += _dot(act, wd_ref[...])

    @pl.when(f == pl.num_programs(1) - 1)
    def _():
        o_ref[...] = acc_ref[...]


def ffn(x, g, wg, wu, wd, *, tm=1024, tf=256):
    m, d = x.shape
    f = wg.shape[1]
    tm, tf = _tile(m, tm), _tile(f, tf)
    return pl.pallas_call(
        _ffn_kernel,
        grid=(m // tm, f // tf),
        in_specs=[pl.BlockSpec((tm, d), lambda i, j: (i, 0)),
                  pl.BlockSpec((1, d), lambda i, j: (0, 0)),
                  pl.BlockSpec((d, tf), lambda i, j: (0, j)),
                  pl.BlockSpec((d, tf), lambda i, j: (0, j)),
                  pl.BlockSpec((tf, d), lambda i, j: (j, 0))],
        out_specs=pl.BlockSpec((tm, d), lambda i, j: (i, 0)),
        out_shape=jax.ShapeDtypeStruct((m, d), F32),
        scratch_shapes=[pltpu.VMEM((tm, d), BF16), pltpu.VMEM((tm, d), F32)],
        compiler_params=_cparams(("parallel", "arbitrary")),
    )(x, g.reshape(1, d), wg, wu, wd)


def _qkv_kernel(x_ref, g_ref, w_ref, pos_ref, gain_ref, freq_ref, bd_ref, o_ref,
                h_ref, cos_ref, sina_ref, sinb_ref, *, n_rot_tiles, tn):
    j = pl.program_id(1)

    @pl.when(j == 0)
    def _():
        h_ref[...] = _rms(x_ref[...], g_ref[...]).astype(BF16)
        ang = pos_ref[...] * freq_ref[...]
        seg = lax.broadcasted_iota(jnp.int32, ang.shape, 1) % HEAD
        half = ROPE_DIM // 2
        cos_ref[...] = jnp.where(seg < ROPE_DIM, jnp.cos(ang), 1.0)
        sin = jnp.sin(ang)
        sina_ref[...] = jnp.where(seg < half, -sin, 0.0)
        sinb_ref[...] = jnp.where((seg >= half) & (seg < ROPE_DIM), sin, 0.0)

    y = _dot(h_ref[...], w_ref[...])

    @pl.when(j < n_rot_tiles)
    def _():
        half = ROPE_DIM // 2
        for c in range(tn // LANES):
            sl = slice(c * LANES, (c + 1) * LANES)
            yc = y[:, sl]
            ms = _segsum(yc * yc, bd_ref[...]) * (1.0 / HEAD)
            yn = yc * lax.rsqrt(ms + RMS_EPS) * gain_ref[:, sl]
            rot = (yn * cos_ref[...] + pltpu.roll(yn, LANES - half, axis=1) * sina_ref[...]
                   + pltpu.roll(yn, half, axis=1) * sinb_ref[...])
            o_ref[:, sl] = rot.astype(o_ref.dtype)

    @pl.when(j >= n_rot_tiles)
    def _():
        o_ref[...] = y.astype(o_ref.dtype)


def qkv_project(x, g, w, pos, gains, freq, bd128, n_rot_cols, *, tm=1024, tn=512):
    m, d = x.shape
    n = w.shape[1]
    tm, tn = _tile(m, tm), _tile(n, tn)
    assert n_rot_cols % tn == 0
    return pl.pallas_call(
        functools.partial(_qkv_kernel, n_rot_tiles=n_rot_cols // tn, tn=tn),
        grid=(m // tm, n // tn),
        in_specs=[pl.BlockSpec((tm, d), lambda i, j: (i, 0)),
                  pl.BlockSpec((1, d), lambda i, j: (0, 0)),
                  pl.BlockSpec((d, tn), lambda i, j: (0, j)),
                  pl.BlockSpec((tm, 1), lambda i, j: (i, 0)),
                  pl.BlockSpec((1, tn), lambda i, j: (0, j)),
                  pl.BlockSpec((1, LANES), lambda i, j: (0, 0)),
                  pl.BlockSpec((LANES, LANES), lambda i, j: (0, 0))],
        out_specs=pl.BlockSpec((tm, tn), lambda i, j: (i, j)),
        out_shape=jax.ShapeDtypeStruct((m, n), BF16),
        scratch_shapes=[pltpu.VMEM((tm, d), BF16), pltpu.VMEM((tm, LANES), F32),
                        pltpu.VMEM((tm, LANES), F32), pltpu.VMEM((tm, LANES), F32)],
        compiler_params=_cparams(("parallel", "arbitrary")),
    )(x, g.reshape(1, d), w, pos, gains, freq, bd128)


def _attn_kernel(fast_ref, q_ref, k_ref, v_ref, lam_ref, subln_ref, o_ref, m_ref, ls_ref, l_ref, acc_ref,
                 *, tq, lambda_init):
    qi = pl.program_id(2)
    q = q_ref[...]
    lane = lax.broadcasted_iota(jnp.int32, q.shape, 1)
    zero = jnp.zeros_like(q)
    qs = jnp.concatenate([jnp.where(lane < HEAD, q, zero), jnp.where(lane < HEAD, zero, q)], axis=0)
    acc_ref[...] = jnp.zeros_like(acc_ref)

    def scores(j, masked):
        start = pl.multiple_of(j * tq, tq)
        s = _nt(qs, k_ref[pl.ds(start, tq), :])
        if masked:
            row = lax.broadcasted_iota(jnp.int32, (tq, tq), 0)
            col = lax.broadcasted_iota(jnp.int32, (tq, tq), 1)
            keep = jnp.concatenate([col <= row, col <= row], axis=0)
            s = jnp.where(keep, s, NEG_BIG)
        return s, v_ref[pl.ds(start, tq), :]

    def sweep(step):
        def body(j, carry):
            step(j, False)
            return carry
        lax.fori_loop(0, qi, body, 0)
        step(qi, True)

    @pl.when(fast_ref[0] == 1)
    def _():
        l_ref[...] = jnp.zeros_like(l_ref)

        def step(j, masked):
            s, vb = scores(j, masked)
            p = jnp.exp2(s)
            part = p[:, 0:LANES]
            for c in range(1, tq // LANES):
                part = part + p[:, c * LANES:(c + 1) * LANES]
            l_ref[...] += part
            acc_ref[...] += _dot(p.astype(BF16), vb)

        sweep(step)
        ls_ref[...] = jnp.sum(l_ref[...], axis=-1, keepdims=True)

    @pl.when(fast_ref[0] == 0)
    def _():
        m_ref[...] = jnp.full_like(m_ref, NEG_BIG)
        ls_ref[...] = jnp.zeros_like(ls_ref)

        def step(j, masked):
            s, vb = scores(j, masked)
            m_old = m_ref[...]
            m_new = jnp.maximum(m_old, jnp.max(s, axis=-1, keepdims=True))
            alpha = jnp.exp2(m_old - m_new)
            p = jnp.exp2(s - m_new)
            ls_ref[...] = alpha * ls_ref[...] + jnp.sum(p, axis=-1, keepdims=True)
            acc_ref[...] = alpha * acc_ref[...] + _dot(p.astype(BF16), vb)
            m_ref[...] = m_new

        sweep(step)

    lq = lam_ref[...]
    lam = (jnp.exp(jnp.sum(lq[0:1] * lq[1:2], axis=-1, keepdims=True))
           - jnp.exp(jnp.sum(lq[2:3] * lq[3:4], axis=-1, keepdims=True)) + lambda_init)
    o = acc_ref[...] / ls_ref[...]
    o = o[:tq] - lam * o[tq:]
    o = _rms(o, subln_ref[...]) * (1.0 - lambda_init)
    o_ref[...] = o.astype(o_ref.dtype)


def diff_attention(qkv, fast, batch, seq, n_heads, lam_params, subln, lambda_init, *, tq=512):
    m = qkv.shape[0]
    tq = _tile(seq, tq)
    nq = seq // tq
    grid_spec = pltpu.PrefetchScalarGridSpec(
        num_scalar_prefetch=1,
        grid=(batch, n_heads, nq),
        in_specs=[pl.BlockSpec((tq, LANES), lambda b, h, i, f: (b * nq + i, h)),
                  pl.BlockSpec((seq, LANES), lambda b, h, i, f: (b, n_heads + h)),
                  pl.BlockSpec((seq, LANES), lambda b, h, i, f: (b, 2 * n_heads + h)),
                  pl.BlockSpec((4, HEAD), lambda b, h, i, f: (0, 0)),
                  pl.BlockSpec((1, LANES), lambda b, h, i, f: (0, 0))],
        out_specs=pl.BlockSpec((tq, LANES), lambda b, h, i, f: (b * nq + i, h)),
        scratch_shapes=[pltpu.VMEM((2 * tq, 1), F32), pltpu.VMEM((2 * tq, 1), F32),
                        pltpu.VMEM((2 * tq, LANES), F32), pltpu.VMEM((2 * tq, LANES), F32)])
    return pl.pallas_call(
        functools.partial(_attn_kernel, tq=tq, lambda_init=lambda_init),
        grid_spec=grid_spec,
        out_shape=jax.ShapeDtypeStruct((m, n_heads * LANES), BF16),
        compiler_params=_cparams(("parallel", "parallel", "arbitrary")),
    )(fast, qkv, qkv, qkv, lam_params, subln.reshape(1, LANES))


def _moe_kernel(x_ref, g_ref, router_ref, wg_ref, wu_ref, wd_ref, o_ref, h_ref, comb_ref, acc_ref):
    e = pl.program_id(1)
    f = pl.program_id(2)

    @pl.when((e == 0) & (f == 0))
    def _():
        x = x_ref[...]
        h = _rms(x, g_ref[...])
        h_ref[...] = h.astype(BF16)
        acc_ref[...] = x
        logits = _dot(h, router_ref[...], precision=HIGHEST)
        lane = lax.broadcasted_iota(jnp.int32, logits.shape, 1)
        logits = jnp.where(lane < N_EXPERTS, logits, NEG_BIG)
        v1 = jnp.max(logits, axis=-1, keepdims=True)
        i1 = jnp.min(jnp.where(logits == v1, lane, LANES), axis=-1, keepdims=True)
        rest = jnp.where(lane == i1, NEG_BIG, logits)
        v2 = jnp.max(rest, axis=-1, keepdims=True)
        i2 = jnp.min(jnp.where(rest == v2, lane, LANES), axis=-1, keepdims=True)
        e2 = jnp.exp(v2 - v1)
        g1 = 1.0 / (1.0 + e2)
        g2 = e2 / (1.0 + e2)
        comb_ref[...] = jnp.where(lane == i1, g1, 0.0) + jnp.where(lane == i2, g2, 0.0)

    comb = comb_ref[...]
    lane = lax.broadcasted_iota(jnp.int32, comb.shape, 1)
    ce = jnp.sum(jnp.where(lane == e, comb, 0.0), axis=-1, keepdims=True)

    h = h_ref[...]
    gate = _dot(h, wg_ref[...])
    up = _dot(h, wu_ref[...])
    act = (gate * _sigmoid(gate) * up * ce).astype(BF16)
    acc_ref[...] += _dot(act, wd_ref[...])

    @pl.when((e == pl.num_programs(1) - 1) & (f == pl.num_programs(2) - 1))
    def _():
        o_ref[...] = acc_ref[...]


def moe(x, g, router_p, wg, wu, wd, *, tm=1024, tf=512):
    m, d = x.shape
    ne, _, f = wg.shape
    tm, tf = _tile(m, tm), _tile(f, tf)
    return pl.pallas_call(
        _moe_kernel,
        grid=(m // tm, ne, f // tf),
        in_specs=[pl.BlockSpec((tm, d), lambda i, e, j: (i, 0)),
                  pl.BlockSpec((1, d), lambda i, e, j: (0, 0)),
                  pl.BlockSpec((d, LANES), lambda i, e, j: (0, 0)),
                  pl.BlockSpec((None, d, tf), lambda i, e, j: (e, 0, j)),
                  pl.BlockSpec((None, d, tf), lambda i, e, j: (e, 0, j)),
                  pl.BlockSpec((None, tf, d), lambda i, e, j: (e, j, 0))],
        out_specs=pl.BlockSpec((tm, d), lambda i, e, j: (i, 0)),
        out_shape=jax.ShapeDtypeStruct((m, d), F32),
        scratch_shapes=[pltpu.VMEM((tm, d), BF16), pltpu.VMEM((tm, LANES), F32),
                        pltpu.VMEM((tm, d), F32)],
        compiler_params=_cparams(("parallel", "arbitrary", "arbitrary")),
    )(x, g.reshape(1, d), router_p, wg, wu, wd)


def _block_diag(blocks):
    n, h, _ = blocks.shape
    eye = jnp.eye(n, dtype=blocks.dtype)
    return (eye[:, None, :, None] * blocks[:, :, None, :]).reshape(n * h, n * h)


def _seg_ones(n):
    seg = jnp.arange(n) // HEAD
    return (seg[:, None] == seg[None, :]).astype(BF16)


def _even_layer(x, batch, seq, ln_mix, w_in, conv_w, conv_b, gate_a_w, gate_a_b, gate_x_w, gate_x_b,
                lru_lambda, shift_mu, w0, w2, a0, a2, g2, k_k, k_a, r_k, gn_w, gn_b, w_out,
                ln_ffn, ffn_gate, ffn_up, ffn_down):
    lru_w = lru_lambda.shape[0]
    rw_w = w0.shape[0]
    dl, al, gl = w2.shape[0], a2.shape[0], g2.shape[0]
    n_in = w_in.shape[1]
    proj = norm_matmul(x, ln_mix, w_in.astype(BF16), tn=n_in // 2)

    y_lru = lru_branch(proj, batch, seq, conv_w, conv_b,
                       _block_diag(gate_a_w).astype(BF16), gate_a_b,
                       _block_diag(gate_x_w).astype(BF16), gate_x_b, lru_lambda)

    lora = dl + al + gl
    zeros = lambda n: jnp.zeros((n, rw_w), F32)
    w2p = jnp.concatenate([w2, zeros(al + gl)], axis=0).astype(BF16)
    a2p = jnp.concatenate([zeros(dl), a2, zeros(gl)], axis=0).astype(BF16)
    g2p = jnp.concatenate([zeros(dl + al), g2], axis=0).astype(BF16)
    assert lora == w2p.shape[0]
    r, k, v, a, b, lw, g = rwkv_prep(proj, seq, 2 * lru_w, shift_mu, w0, w2p, a0, a2p, g2p, k_k, k_a,
                                     _seg_ones(rw_w))
    y_rwkv = rwkv_scan(r, k, v, a, b, lw, g, batch, seq, r_k.reshape(-1), gn_w, gn_b, _seg_ones(LANES))

    w_out = w_out.astype(BF16)
    x = matmul_residual([y_lru, y_rwkv], [w_out[:lru_w], w_out[lru_w:]], x)
    return ffn(x, ln_ffn, ffn_gate.astype(BF16), ffn_up.astype(BF16), ffn_down.astype(BF16))


def _odd_layer(x, pos, batch, seq, layer_idx, ln_mix, w_qkv, q_norm, k_norm, lq1, lk1, lq2, lk2, subln,
               w_o, ln_ffn, router, moe_gate, moe_up, moe_down):
    d = x.shape[1]
    n_heads = d // (2 * HEAD)
    qd = n_heads * 2 * HEAD
    lambda_init = 0.8 - 0.6 * math.exp(-0.3 * layer_idx)
    reps = qd // HEAD
    q_gain = q_norm * (HEAD ** -0.5 * LOG2E)
    logit_bound = 1.02 * HEAD * jnp.max(jnp.abs(q_gain)) * jnp.max(jnp.abs(k_norm))
    fast = (logit_bound <= MAX_EXP2_ARG).astype(jnp.int32).reshape(1)
    gains = jnp.concatenate([jnp.tile(q_gain, reps),
                             jnp.tile(k_norm, reps),
                             jnp.ones((w_qkv.shape[1] - 2 * qd,), F32)])[None, :]
    seg = jnp.arange(LANES) % HEAD
    inv_freq = ROPE_THETA ** (-(2.0 * (seg % (ROPE_DIM // 2))).astype(F32) / ROPE_DIM)
    freq = jnp.where(seg < ROPE_DIM, inv_freq, 0.0)[None, :].astype(F32)
    qkv = qkv_project(x, ln_mix, w_qkv.astype(BF16), pos, gains, freq, _seg_ones(LANES), 2 * qd)
    lam_params = jnp.stack([lq1, lk1, lq2, lk2]).astype(F32)
    o = diff_attention(qkv, fast, batch, seq, n_heads, lam_params, subln, lambda_init)
    x = matmul_residual([o], [w_o.astype(BF16)], x)
    router_p = jnp.pad(router, ((0, 0), (0, LANES - router.shape[1])))
    return moe(x, ln_ffn, router_p, moe_gate.astype(BF16), moe_up.astype(BF16), moe_down.astype(BF16))


def kernel(x, positions, e_ln_mix, e_w_in, e_conv_w, e_conv_b, e_gate_a_w, e_gate_a_b, e_gate_x_w, e_gate_x_b, e_lru_lambda, e_shift_mu, e_w0, e_w2, e_a0, e_a2, e_g2, e_k_k, e_k_a, e_r_k, e_gn_w, e_gn_b, e_w_out, e_ln_ffn, e_ffn_gate, e_ffn_up, e_ffn_down, o_ln_mix, o_w_qkv, o_q_norm, o_k_norm, o_lambda_q1, o_lambda_k1, o_lambda_q2, o_lambda_k2, o_subln, o_w_o, o_ln_ffn, o_router, o_moe_gate, o_moe_up, o_moe_down):
    batch, seq, d = x.shape
    depth = e_ln_mix.shape[0] + o_ln_mix.shape[0]
    xf = x.reshape(batch * seq, d)
    pos = positions.reshape(batch * seq, 1).astype(F32)
    for i in range(depth):
        j = i // 2
        if i % 2 == 0:
            xf = _even_layer(xf, batch, seq, e_ln_mix[j], e_w_in[j], e_conv_w[j], e_conv_b[j],
                             e_gate_a_w[j], e_gate_a_b[j], e_gate_x_w[j], e_gate_x_b[j],
                             e_lru_lambda[j], e_shift_mu[j], e_w0[j], e_w2[j], e_a0[j], e_a2[j],
                             e_g2[j], e_k_k[j], e_k_a[j], e_r_k[j], e_gn_w[j], e_gn_b[j], e_w_out[j],
                             e_ln_ffn[j], e_ffn_gate[j], e_ffn_up[j], e_ffn_down[j])
        else:
            xf = _odd_layer(xf, pos, batch, seq, i, o_ln_mix[j], o_w_qkv[j], o_q_norm[j], o_k_norm[j],
                            o_lambda_q1[j], o_lambda_k1[j], o_lambda_q2[j], o_lambda_k2[j], o_subln[j],
                            o_w_o[j], o_ln_ffn[j], o_router[j], o_moe_gate[j], o_moe_up[j],
                            o_moe_down[j])
    return xf.reshape(batch, seq, d)
```

```python
import functools
import math

import jax
import jax.numpy as jnp
from jax import lax
from jax.experimental import pallas as pl
from jax.experimental.pallas import tpu as pltpu

F32 = jnp.float32
BF16 = jnp.bfloat16
HIGHEST = lax.Precision.HIGHEST

LANES = 128
VMEM_LIMIT = 56 * 1024 * 1024

HEAD = 64
CONV_WIDTH = 4
LRU_C = 8.0
GN_EPS = 64e-5
RMS_EPS = 1e-6
ROPE_DIM = 16
ROPE_THETA = 500000.0
N_EXPERTS = 8
CHUNK = 64
NEG_BIG = -1e30
LOG2E = 1.4426950408889634
MAX_EXP2_ARG = 60.0


def _cparams(sem):
    return pltpu.CompilerParams(dimension_semantics=sem, vmem_limit_bytes=VMEM_LIMIT)


def _tile(n, pref):
    t = min(n, pref)
    assert n % t == 0, (n, pref)
    return t


def _nt(a, b, **kw):
    return lax.dot_general(a, b, (((1,), (1,)), ((), ())), preferred_element_type=F32, **kw)


def _tn(a, b, **kw):
    return lax.dot_general(a, b, (((0,), (0,)), ((), ())), preferred_element_type=F32, **kw)


def _dot(a, b, **kw):
    return jnp.dot(a, b, preferred_element_type=F32, **kw)


def _segsum(x, bd):
    hi = x.astype(BF16)
    lo = (x - hi.astype(F32)).astype(BF16)
    return _dot(hi, bd) + _dot(lo, bd)


def _rms(x, g):
    ms = jnp.mean(x * x, axis=-1, keepdims=True)
    return x * lax.rsqrt(ms + RMS_EPS) * g


def _sigmoid(x):
    return 1.0 / (1.0 + jnp.exp(-x))


def _softplus(x):
    return jnp.maximum(x, 0.0) + jnp.log1p(jnp.exp(-jnp.abs(x)))


def _norm_mm_kernel(x_ref, g_ref, w_ref, o_ref, h_ref):
    @pl.when(pl.program_id(1) == 0)
    def _():
        h_ref[...] = _rms(x_ref[...], g_ref[...]).astype(BF16)

    o_ref[...] = _dot(h_ref[...], w_ref[...]).astype(o_ref.dtype)


def norm_matmul(x, g, w, *, tm=1024, tn=512, out_dtype=F32):
    m, d = x.shape
    n = w.shape[1]
    tm, tn = _tile(m, tm), _tile(n, tn)
    return pl.pallas_call(
        _norm_mm_kernel,
        grid=(m // tm, n // tn),
        in_specs=[pl.BlockSpec((tm, d), lambda i, j: (i, 0)),
                  pl.BlockSpec((1, d), lambda i, j: (0, 0)),
                  pl.BlockSpec((d, tn), lambda i, j: (0, j))],
        out_specs=pl.BlockSpec((tm, tn), lambda i, j: (i, j)),
        out_shape=jax.ShapeDtypeStruct((m, n), out_dtype),
        scratch_shapes=[pltpu.VMEM((tm, d), BF16)],
        compiler_params=_cparams(("parallel", "arbitrary")),
    )(x, g.reshape(1, d), w)


def _mm_res_kernel(*refs, n_in):
    ys, ws = refs[:n_in], refs[n_in:2 * n_in]
    res_ref, o_ref = refs[2 * n_in], refs[2 * n_in + 1]
    acc = res_ref[...]
    for y_ref, w_ref in zip(ys, ws):
        acc = acc + _dot(y_ref[...], w_ref[...])
    o_ref[...] = acc


def matmul_residual(ys, ws, res, *, tm=1024, tn=512):
    m, n = res.shape
    tm, tn = _tile(m, tm), _tile(n, tn)
    n_in = len(ys)
    in_specs = [pl.BlockSpec((tm, y.shape[1]), lambda i, j: (i, 0)) for y in ys]
    in_specs += [pl.BlockSpec((w.shape[0], tn), lambda i, j: (0, j)) for w in ws]
    in_specs += [pl.BlockSpec((tm, tn), lambda i, j: (i, j))]
    return pl.pallas_call(
        functools.partial(_mm_res_kernel, n_in=n_in),
        grid=(m // tm, n // tn),
        in_specs=in_specs,
        out_specs=pl.BlockSpec((tm, tn), lambda i, j: (i, j)),
        out_shape=jax.ShapeDtypeStruct((m, n), F32),
        compiler_params=_cparams(("parallel", "arbitrary")),
    )(*ys, *ws, res)


def _lru_kernel(x_ref, gate_ref, cw_ref, cb_ref, wa_ref, ba_ref, wx_ref, bx_ref, lam_ref,
                o_ref, tail_ref, h_ref, *, tm):
    @pl.when(pl.program_id(1) == 0)
    def _():
        tail_ref[...] = jnp.zeros_like(tail_ref)
        h_ref[...] = jnp.zeros_like(h_ref)

    x = x_ref[...]
    xx = jnp.concatenate([tail_ref[...], x], axis=0)
    tail_ref[...] = x[tm - 8:, :]
    cw = cw_ref[...]
    xc = cb_ref[...] + cw[CONV_WIDTH - 1:CONV_WIDTH, :] * x
    for s in range(1, CONV_WIDTH):
        xc = xc + cw[CONV_WIDTH - 1 - s:CONV_WIDTH - s, :] * pltpu.roll(xx, s, axis=0)[8:, :]

    xb = xc.astype(BF16)
    gate_a = _dot(xb, wa_ref[...]) + ba_ref[...]
    gate_x = _dot(xb, wx_ref[...]) + bx_ref[...]
    log_a = -LRU_C * _sigmoid(gate_a) * _softplus(-lam_ref[...])
    a = jnp.exp(log_a)
    u = jnp.sqrt(1.0 - a * a) * _sigmoid(gate_x) * xc

    row = lax.broadcasted_iota(jnp.int32, a.shape, 0)
    s = 1
    while s < tm:
        keep = row >= s
        a_s = jnp.where(keep, pltpu.roll(a, s, axis=0), 1.0)
        u_s = jnp.where(keep, pltpu.roll(u, s, axis=0), 0.0)
        u = a * u_s + u
        a = a * a_s
        s *= 2
    h = a * h_ref[...] + u
    h_ref[...] = h[tm - 1:tm, :]
    o_ref[...] = (jax.nn.gelu(gate_ref[...]) * h).astype(o_ref.dtype)


def lru_branch(proj, batch, seq, conv_w, conv_b, wa_bd, ba, wx_bd, bx, lam, *, tm=256):
    m = proj.shape[0]
    w = lam.shape[0]
    tm = _tile(seq, tm)
    nt = seq // tm
    vec = lambda: pl.BlockSpec((1, w), lambda b, i: (0, 0))
    return pl.pallas_call(
        functools.partial(_lru_kernel, tm=tm),
        grid=(batch, nt),
        in_specs=[pl.BlockSpec((tm, w), lambda b, i: (b * nt + i, 0)),
                  pl.BlockSpec((tm, w), lambda b, i: (b * nt + i, 1)),
                  pl.BlockSpec((CONV_WIDTH, w), lambda b, i: (0, 0)),
                  vec(),
                  pl.BlockSpec((w, w), lambda b, i: (0, 0)), vec(),
                  pl.BlockSpec((w, w), lambda b, i: (0, 0)), vec(),
                  vec()],
        out_specs=pl.BlockSpec((tm, w), lambda b, i: (b * nt + i, 0)),
        out_shape=jax.ShapeDtypeStruct((m, w), BF16),
        scratch_shapes=[pltpu.VMEM((8, w), F32), pltpu.VMEM((1, w), F32)],
        compiler_params=_cparams(("parallel", "arbitrary")),
    )(proj, proj, conv_w, conv_b.reshape(1, w), wa_bd, ba.reshape(1, w), wx_bd, bx.reshape(1, w),
      lam.reshape(1, w))


def _rwkv_prep_kernel(pr_ref, pk_ref, pv_ref, pl_ref, qr_ref, qk_ref, qv_ref, ql_ref,
                      mur_ref, muk_ref, muv_ref, mul_ref, w0_ref, w2_ref, a0_ref, a2_ref, g2_ref,
                      kk_ref, ka_ref, bd_ref,
                      r_out, k_out, v_out, a_out, b_out, lw_out, g_out, *, tm, seq):
    first = (pl.program_id(0) * tm) % seq == 0

    def shift_mix(p_ref, q_ref, mu_ref):
        x = p_ref[...]
        prev = jnp.where(first, 0.0, q_ref[7:8, :])
        row = lax.broadcasted_iota(jnp.int32, x.shape, 0)
        xs = jnp.where(row == 0, prev, pltpu.roll(x, 1, axis=0))
        return x + (xs - x) * mu_ref[...]

    r = shift_mix(pr_ref, qr_ref, mur_ref)
    k = shift_mix(pk_ref, qk_ref, muk_ref)
    v = shift_mix(pv_ref, qv_ref, muv_ref)
    lo = shift_mix(pl_ref, ql_ref, mul_ref)

    wlog = -_softplus(-(w0_ref[...] + _dot(jnp.tanh(lo).astype(BF16), w2_ref[...]))) - 0.5
    a = _sigmoid(a0_ref[...] + _dot(lo.astype(BF16), a2_ref[...]))
    g = _dot(_sigmoid(lo).astype(BF16), g2_ref[...])

    kk = k * kk_ref[...]
    nrm = jnp.sqrt(_segsum(kk * kk, bd_ref[...]))
    kk = kk / jnp.maximum(nrm, 1e-12)

    r_out[...] = r
    k_out[...] = k * (1.0 + (a - 1.0) * ka_ref[...])
    v_out[...] = v
    a_out[...] = -kk
    b_out[...] = kk * a
    lw_out[...] = -jnp.exp(wlog)
    g_out[...] = g


def rwkv_prep(proj, seq, col0, mu, w0, w2p, a0, a2p, g2p, k_k, k_a, bd, *, tm=512):
    m = proj.shape[0]
    w = w0.shape[0]
    lw = w2p.shape[0]
    tm = _tile(seq, tm)
    cb = col0 // w
    lb = (col0 + 3 * w) // lw
    main = lambda c, width: pl.BlockSpec((tm, width), lambda i: (i, c))
    prev = lambda c, width: pl.BlockSpec((8, width), lambda i: (jnp.maximum(i * (tm // 8) - 1, 0), c))
    vec = lambda width: pl.BlockSpec((1, width), lambda i: (0, 0))
    mat = lambda a: pl.BlockSpec(a.shape, lambda i: (0, 0))
    mu_r, mu_k, mu_v, mu_l = (mu[None, 0:w], mu[None, w:2 * w], mu[None, 2 * w:3 * w], mu[None, 3 * w:])
    outs = [jax.ShapeDtypeStruct((m, w), F32)] * 7
    return pl.pallas_call(
        functools.partial(_rwkv_prep_kernel, tm=tm, seq=seq),
        grid=(m // tm,),
        in_specs=[main(cb, w), main(cb + 1, w), main(cb + 2, w), main(lb, lw),
                  prev(cb, w), prev(cb + 1, w), prev(cb + 2, w), prev(lb, lw),
                  vec(w), vec(w), vec(w), vec(lw),
                  vec(w), mat(w2p), vec(w), mat(a2p), mat(g2p), vec(w), vec(w), mat(bd)],
        out_specs=[pl.BlockSpec((tm, w), lambda i: (i, 0))] * 7,
        out_shape=outs,
        compiler_params=_cparams(("parallel",)),
    )(proj, proj, proj, proj, proj, proj, proj, proj, mu_r, mu_k, mu_v, mu_l,
      w0.reshape(1, w), w2p, a0.reshape(1, w), a2p, g2p, k_k.reshape(1, w), k_a.reshape(1, w), bd)


def _rwkv_scan_kernel(r_ref, k_ref, v_ref, a_ref, b_ref, lw_ref, g_ref, rk_ref, gnw_ref, gnb_ref,
                      bd_ref, o_ref, s_ref, *, batch, width):
    @pl.when(pl.program_id(0) == 0)
    def _():
        s_ref[...] = jnp.zeros_like(s_ref)

    c = CHUNK
    tri = (lax.broadcasted_iota(jnp.int32, (c, c), 0) >= lax.broadcasted_iota(jnp.int32, (c, c), 1))
    lane = lax.broadcasted_iota(jnp.int32, (c, LANES), 1)
    head0 = lane < HEAD
    i2 = lax.broadcasted_iota(jnp.int32, (2 * c, 2 * c), 0)
    j2 = lax.broadcasted_iota(jnp.int32, (2 * c, 2 * c), 1)
    strict = i2 > j2
    incl = i2 >= j2
    bd = bd_ref[...]
    n_steps = int(math.log2(c))

    def stack(x):
        xb = x.astype(BF16)
        zero = jnp.zeros_like(xb)
        return jnp.concatenate([jnp.where(head0, xb, zero), jnp.where(head0, zero, xb)], axis=0)

    chains = []
    for bi in range(batch):
        lw = lw_ref[bi]
        cum = _dot(tri.astype(F32), lw, precision=HIGHEST)
        tot = cum[c - 1:c, :]
        g_out = jnp.exp(-cum)
        g_suf = jnp.exp(tot - cum)
        g_tot = jnp.exp(tot)
        r, k, v, b = r_ref[bi], k_ref[bi], v_ref[bi], b_ref[bi]
        rt = r * jnp.exp(cum)
        kt = k * g_out
        at = a_ref[bi] * jnp.exp(cum - lw)
        bt = b * g_out
        ks = k * g_suf
        bs = b * g_suf
        rkv = r * k * rk_ref[...]
        for p in range(width // LANES):
            sl = slice(p * LANES, (p + 1) * LANES)
            chains.append(dict(
                bi=bi, p=p, sl=sl, v=v[:, sl], rkv=rkv[:, sl], g_tot=g_tot[:, sl],
                v_s=stack(v[:, sl]),
                ar=jnp.concatenate([stack(at[:, sl]), stack(rt[:, sl])], axis=0),
                kb=jnp.concatenate([stack(kt[:, sl]), stack(bt[:, sl])], axis=0),
                suf=jnp.concatenate([stack(ks[:, sl]), stack(bs[:, sl])], axis=0)))

    for ch in chains:
        ch["state"] = s_ref[ch["bi"], ch["p"]]
        ch["gram"] = _nt(ch["ar"], ch["kb"])
        ch["xs"] = _nt(ch["ar"], ch["state"].astype(BF16))
    for ch in chains:
        gram = ch["gram"]
        ak = jnp.where(strict, gram[:2 * c, :2 * c], 0.0)
        rk = jnp.where(incl, gram[2 * c:, :2 * c], 0.0)
        ch["lpow"] = jnp.where(strict, gram[:2 * c, 2 * c:], 0.0)
        ch["rb"] = jnp.where(incl, gram[2 * c:, 2 * c:], 0.0).astype(BF16)
        ch["lv"] = _dot(jnp.concatenate([ak, rk], axis=0).astype(BF16), ch["v_s"])
    for ch in chains:
        ch["u"] = ch["xs"][:2 * c] + ch["lv"][:2 * c]
    for i in range(n_steps):
        for ch in chains:
            lb = ch["lpow"].astype(BF16)
            ch["u"] = ch["u"] + _dot(lb, ch["u"].astype(BF16))
            if i + 1 < n_steps:
                ch["lpow"] = _dot(lb, lb)
    for ch in chains:
        ub = ch["u"].astype(BF16)
        o_s = ch["xs"][2 * c:] + ch["lv"][2 * c:] + _dot(ch["rb"], ub)
        ch["o"] = o_s[:c] + o_s[c:]
        upd = _tn(jnp.concatenate([ch["v_s"], ub], axis=0), ch["suf"])
        s_ref[ch["bi"], ch["p"]] = ch["state"] * ch["g_tot"] + upd
    for ch in chains:
        bi, sl, o = ch["bi"], ch["sl"], ch["o"]
        mean = _segsum(o, bd) * (1.0 / HEAD)
        d = o - mean
        var = _segsum(d * d, bd) * (1.0 / HEAD)
        on = d * lax.rsqrt(var + GN_EPS) * gnw_ref[:, sl] + gnb_ref[:, sl]
        bonus = _segsum(ch["rkv"], bd) * ch["v"]
        o_ref[bi, :, sl] = ((on + bonus) * g_ref[bi, :, sl]).astype(o_ref.dtype)


def rwkv_scan(r, k, v, a, b, lw, g, batch, seq, r_k, gn_w, gn_b, bd128):
    m, w = r.shape
    nc = seq // CHUNK
    blk = lambda: pl.BlockSpec((batch, CHUNK, w), lambda ci: (0, ci, 0))
    vec = lambda: pl.BlockSpec((1, w), lambda ci: (0, 0))
    as3d = lambda x: x.reshape(batch, seq, w)
    out = pl.pallas_call(
        functools.partial(_rwkv_scan_kernel, batch=batch, width=w),
        grid=(nc,),
        in_specs=[blk() for _ in range(7)] + [vec(), vec(), vec(),
                                               pl.BlockSpec((LANES, LANES), lambda ci: (0, 0))],
        out_specs=blk(),
        out_shape=jax.ShapeDtypeStruct((batch, seq, w), BF16),
        scratch_shapes=[pltpu.VMEM((batch, w // LANES, LANES, LANES), F32)],
        compiler_params=_cparams(("arbitrary",)),
    )(as3d(r), as3d(k), as3d(v), as3d(a), as3d(b), as3d(lw), as3d(g),
      r_k.reshape(1, w), gn_w.reshape(1, w), gn_b.reshape(1, w), bd128)
    return out.reshape(m, w)


def _ffn_kernel(x_ref, g_ref, wg_ref, wu_ref, wd_ref, o_ref, h_ref, acc_ref):
    f = pl.program_id(1)

    @pl.when(f == 0)
    def _():
        h_ref[...] = _rms(x_ref[...], g_ref[...]).astype(BF16)
        acc_ref[...] = x_ref[...]

    h = h_ref[...]
    gate = _dot(h, wg_ref[...])
    up = _dot(h, wu_ref[...])
    act = (gate * _sigmoid(gate) * up).astype(BF16)
    acc_ref[...] += _dot(act, wd_ref[...])

    @pl.when(f == pl.num_programs(1) - 1)
    def _():
        o_ref[...] = acc_ref[...]


def ffn(x, g, wg, wu, wd, *, tm=1024, tf=256):
    m, d = x.shape
    f = wg.shape[1]
    tm, tf = _tile(m, tm), _tile(f, tf)
    return pl.pallas_call(
        _ffn_kernel,
        grid=(m // tm, f // tf),
        in_specs=[pl.BlockSpec((tm, d), lambda i, j: (i, 0)),
                  pl.BlockSpec((1, d), lambda i, j: (0, 0)),
                  pl.BlockSpec((d, tf), lambda i, j: (0, j)),
                  pl.BlockSpec((d, tf), lambda i, j: (0, j)),
                  pl.BlockSpec((tf, d), lambda i, j: (j, 0))],
        out_specs=pl.BlockSpec((tm, d), lambda i, j: (i, 0)),
        out_shape=jax.ShapeDtypeStruct((m, d), F32),
        scratch_shapes=[pltpu.VMEM((tm, d), BF16), pltpu.VMEM((tm, d), F32)],
        compiler_params=_cparams(("parallel", "arbitrary")),
    )(x, g.reshape(1, d), wg, wu, wd)


def _qkv_kernel(x_ref, g_ref, w_ref, pos_ref, gain_ref, freq_ref, bd_ref, o_ref,
                h_ref, cos_ref, sina_ref, sinb_ref, *, n_rot_tiles, tn):
    j = pl.program_id(1)

    @pl.when(j == 0)
    def _():
        h_ref[...] = _rms(x_ref[...], g_ref[...]).astype(BF16)
        ang = pos_ref[...] * freq_ref[...]
        seg = lax.broadcasted_iota(jnp.int32, ang.shape, 1) % HEAD
        half = ROPE_DIM // 2
        cos_ref[...] = jnp.where(seg < ROPE_DIM, jnp.cos(ang), 1.0)
        sin = jnp.sin(ang)
        sina_ref[...] = jnp.where(seg < half, -sin, 0.0)
        sinb_ref[...] = jnp.where((seg >= half) & (seg < ROPE_DIM), sin, 0.0)

    y = _dot(h_ref[...], w_ref[...])

    @pl.when(j < n_rot_tiles)
    def _():
        half = ROPE_DIM // 2
        for c in range(tn // LANES):
            sl = slice(c * LANES, (c + 1) * LANES)
            yc = y[:, sl]
            ms = _segsum(yc * yc, bd_ref[...]) * (1.0 / HEAD)
            yn = yc * lax.rsqrt(ms + RMS_EPS) * gain_ref[:, sl]
            rot = (yn * cos_ref[...] + pltpu.roll(yn, LANES - half, axis=1) * sina_ref[...]
                   + pltpu.roll(yn, half, axis=1) * sinb_ref[...])
            o_ref[:, sl] = rot.astype(o_ref.dtype)

    @pl.when(j >= n_rot_tiles)
    def _():
        o_ref[...] = y.astype(o_ref.dtype)


def qkv_project(x, g, w, pos, gains, freq, bd128, n_rot_cols, *, tm=1024, tn=512):
    m, d = x.shape
    n = w.shape[1]
    tm, tn = _tile(m, tm), _tile(n, tn)
    assert n_rot_cols % tn == 0
    return pl.pallas_call(
        functools.partial(_qkv_kernel, n_rot_tiles=n_rot_cols // tn, tn=tn),
        grid=(m // tm, n // tn),
        in_specs=[pl.BlockSpec((tm, d), lambda i, j: (i, 0)),
                  pl.BlockSpec((1, d), lambda i, j: (0, 0)),
                  pl.BlockSpec((d, tn), lambda i, j: (0, j)),
                  pl.BlockSpec((tm, 1), lambda i, j: (i, 0)),
                  pl.BlockSpec((1, tn), lambda i, j: (0, j)),
                  pl.BlockSpec((1, LANES), lambda i, j: (0, 0)),
                  pl.BlockSpec((LANES, LANES), lambda i, j: (0, 0))],
        out_specs=pl.BlockSpec((tm, tn), lambda i, j: (i, j)),
        out_shape=jax.ShapeDtypeStruct((m, n), BF16),
        scratch_shapes=[pltpu.VMEM((tm, d), BF16), pltpu.VMEM((tm, LANES), F32),
                        pltpu.VMEM((tm, LANES), F32), pltpu.VMEM((tm, LANES), F32)],
        compiler_params=_cparams(("parallel", "arbitrary")),
    )(x, g.reshape(1, d), w, pos, gains, freq, bd128)


def _attn_kernel(fast_ref, q_ref, k_ref, v_ref, lam_ref, subln_ref, o_ref, m_ref, ls_ref, l_ref, acc_ref,
                 *, tq, lambda_init):
    qi = pl.program_id(2)
    q = q_ref[...]
    lane = lax.broadcasted_iota(jnp.int32, q.shape, 1)
    zero = jnp.zeros_like(q)
    qs = jnp.concatenate([jnp.where(lane < HEAD, q, zero), jnp.where(lane < HEAD, zero, q)], axis=0)
    acc_ref[...] = jnp.zeros_like(acc_ref)

    def scores(j, masked):
        start = pl.multiple_of(j * tq, tq)
        s = _nt(qs, k_ref[pl.ds(start, tq), :])
        if masked:
            row = lax.broadcasted_iota(jnp.int32, (tq, tq), 0)
            col = lax.broadcasted_iota(jnp.int32, (tq, tq), 1)
            keep = jnp.concatenate([col <= row, col <= row], axis=0)
            s = jnp.where(keep, s, NEG_BIG)
        return s, v_ref[pl.ds(start, tq), :]

    def sweep(step):
        def body(j, carry):
            step(j, False)
            return carry
        lax.fori_loop(0, qi, body, 0)
        step(qi, True)

    @pl.when(fast_ref[0] == 1)
    def _():
        l_ref[...] = jnp.zeros_like(l_ref)

        def step(j, masked):
            s, vb = scores(j, masked)
            p = jnp.exp2(s)
            part = p[:, 0:LANES]
            for c in range(1, tq // LANES):
                part = part + p[:, c * LANES:(c + 1) * LANES]
            l_ref[...] += part
            acc_ref[...] += _dot(p.astype(BF16), vb)

        sweep(step)
        ls_ref[...] = jnp.sum(l_ref[...], axis=-1, keepdims=True)

    @pl.when(fast_ref[0] == 0)
    def _():
        m_ref[...] = jnp.full_like(m_ref, NEG_BIG)
        ls_ref[...] = jnp.zeros_like(ls_ref)

        def step(j, masked):
            s, vb = scores(j, masked)
            m_old = m_ref[...]
            m_new = jnp.maximum(m_old, jnp.max(s, axis=-1, keepdims=True))
            alpha = jnp.exp2(m_old - m_new)
            p = jnp.exp2(s - m_new)
            ls_ref[...] = alpha * ls_ref[...] + jnp.sum(p, axis=-1, keepdims=True)
            acc_ref[...] = alpha * acc_ref[...] + _dot(p.astype(BF16), vb)
            m_ref[...] = m_new

        sweep(step)

    lq = lam_ref[...]
    lam = (jnp.exp(jnp.sum(lq[0:1] * lq[1:2], axis=-1, keepdims=True))
           - jnp.exp(jnp.sum(lq[2:3] * lq[3:4], axis=-1, keepdims=True)) + lambda_init)
    o = acc_ref[...] / ls_ref[...]
    o = o[:tq] - lam * o[tq:]
    o = _rms(o, subln_ref[...]) * (1.0 - lambda_init)
    o_ref[...] = o.astype(o_ref.dtype)


def diff_attention(qkv, fast, batch, seq, n_heads, lam_params, subln, lambda_init, *, tq=512):
    m = qkv.shape[0]
    tq = _tile(seq, tq)
    nq = seq // tq
    grid_spec = pltpu.PrefetchScalarGridSpec(
        num_scalar_prefetch=1,
        grid=(batch, n_heads, nq),
        in_specs=[pl.BlockSpec((tq, LANES), lambda b, h, i, f: (b * nq + i, h)),
                  pl.BlockSpec((seq, LANES), lambda b, h, i, f: (b, n_heads + h)),
                  pl.BlockSpec((seq, LANES), lambda b, h, i, f: (b, 2 * n_heads + h)),
                  pl.BlockSpec((4, HEAD), lambda b, h, i, f: (0, 0)),
                  pl.BlockSpec((1, LANES), lambda b, h, i, f: (0, 0))],
        out_specs=pl.BlockSpec((tq, LANES), lambda b, h, i, f: (b * nq + i, h)),
        scratch_shapes=[pltpu.VMEM((2 * tq, 1), F32), pltpu.VMEM((2 * tq, 1), F32),
                        pltpu.VMEM((2 * tq, LANES), F32), pltpu.VMEM((2 * tq, LANES), F32)])
    return pl.pallas_call(
        functools.partial(_attn_kernel, tq=tq, lambda_init=lambda_init),
        grid_spec=grid_spec,
        out_shape=jax.ShapeDtypeStruct((m, n_heads * LANES), BF16),
        compiler_params=_cparams(("parallel", "parallel", "arbitrary")),
    )(fast, qkv, qkv, qkv, lam_params, subln.reshape(1, LANES))


def _router_kernel(x_ref, g_ref, router_ref, ids_ref, gates_ref):
    h = _rms(x_ref[...], g_ref[...])
    logits = _dot(h, router_ref[...], precision=HIGHEST)
    lane = lax.broadcasted_iota(jnp.int32, logits.shape, 1)
    logits = jnp.where(lane < N_EXPERTS, logits, NEG_BIG)
    v1 = jnp.max(logits, axis=-1, keepdims=True)
    i1 = jnp.min(jnp.where(logits == v1, lane, LANES), axis=-1, keepdims=True)
    rest = jnp.where(lane == i1, NEG_BIG, logits)
    v2 = jnp.max(rest, axis=-1, keepdims=True)
    i2 = jnp.min(jnp.where(rest == v2, lane, LANES), axis=-1, keepdims=True)
    e2 = jnp.exp(v2 - v1)
    ids_ref[...] = jnp.where(lane == 0, i1, i2)
    gates_ref[...] = jnp.where(lane == 0, 1.0 / (1.0 + e2), e2 / (1.0 + e2))


def moe_router(x, g, router_p, *, tm=1024):
    m, d = x.shape
    tm = _tile(m, tm)
    return pl.pallas_call(
        _router_kernel,
        grid=(m // tm,),
        in_specs=[pl.BlockSpec((tm, d), lambda i: (i, 0)),
                  pl.BlockSpec((1, d), lambda i: (0, 0)),
                  pl.BlockSpec((d, LANES), lambda i: (0, 0))],
        out_specs=[pl.BlockSpec((tm, LANES), lambda i: (i, 0))] * 2,
        out_shape=[jax.ShapeDtypeStruct((m, LANES), jnp.int32), jax.ShapeDtypeStruct((m, LANES), F32)],
        compiler_params=_cparams(("parallel",)),
    )(x, g.reshape(1, d), router_p)


def _route_tables(ids, tm, n_tiles):
    n_pairs = ids.shape[0] * 2
    n_rows = n_tiles * tm
    shift = max(n_pairs, n_rows).bit_length()
    e_flat = ids.reshape(-1)
    experts = jnp.arange(N_EXPERTS, dtype=jnp.int32)
    counts = jnp.sum((e_flat[:, None] == experts[None, :]).astype(jnp.int32), axis=0)
    padded = ((counts + tm - 1) // tm) * tm
    ends = jnp.cumsum(padded)
    pad_ends = jnp.cumsum(padded - counts)
    q = jnp.arange(n_rows - n_pairs, dtype=jnp.int32)
    pad_expert = jnp.sum((q[:, None] >= pad_ends[None, :]).astype(jnp.int32), axis=1)
    low = (1 << shift) - 1
    keys = jnp.concatenate([(e_flat << shift) | jnp.arange(n_pairs, dtype=jnp.int32),
                            (pad_expert << shift) | low])
    keys = jnp.sort(keys)
    perm = jnp.where((keys & low) == low, -1, keys & low)
    starts = jnp.arange(n_tiles, dtype=jnp.int32) * tm
    tile_expert = jnp.sum((starts[:, None] >= ends[None, :]).astype(jnp.int32), axis=1)
    n_valid = ends[-1] // tm
    last_expert = jnp.sum(jnp.where(jnp.arange(n_tiles) == n_valid - 1, tile_expert, 0))
    tile_expert = jnp.where(jnp.arange(n_tiles) < n_valid, tile_expert, last_expert)
    return perm, tile_expert.astype(jnp.int32), n_valid.reshape(1).astype(jnp.int32)


def _moe_group_kernel(te_ref, nv_ref, perm_ref, x_hbm, g_ref, wg_ref, wu_ref, wd_ref, y_hbm,
                      xbuf, h_ref, acc_ref, gsem, ssem, *, tm):
    i = pl.program_id(0)
    f = pl.program_id(1)
    valid = i < nv_ref[0]
    base = i * tm

    def row_copy_in(r, t):
        return pltpu.make_async_copy(x_hbm.at[pl.ds(t, 1), :], xbuf.at[pl.ds(r, 1), :], gsem)

    def row_copy_out(r, j):
        return pltpu.make_async_copy(acc_ref.at[pl.ds(r, 1), :], y_hbm.at[pl.ds(j, 1), :], ssem)

    @pl.when(valid & (f == 0))
    def _():
        def start(r, carry):
            j = perm_ref[base + r]
            row_copy_in(r, jnp.maximum(j, 0) >> 1).start()
            return carry
        lax.fori_loop(0, tm, start, 0, unroll=8)

        def wait(r, carry):
            row_copy_in(r, 0).wait()
            return carry
        lax.fori_loop(0, tm, wait, 0, unroll=8)
        h_ref[...] = _rms(xbuf[...], g_ref[...]).astype(BF16)
        acc_ref[...] = jnp.zeros_like(acc_ref)

    @pl.when(valid)
    def _():
        h = h_ref[...]
        gate = _dot(h, wg_ref[...])
        up = _dot(h, wu_ref[...])
        act = (gate * _sigmoid(gate) * up).astype(BF16)
        acc_ref[...] += _dot(act, wd_ref[...])

    @pl.when(valid & (f == pl.num_programs(1) - 1))
    def _():
        def start(r, carry):
            j = perm_ref[base + r]

            @pl.when(j >= 0)
            def _():
                row_copy_out(r, j).start()
            return carry
        lax.fori_loop(0, tm, start, 0)

        def wait(r, carry):
            @pl.when(perm_ref[base + r] >= 0)
            def _():
                row_copy_out(r, 0).wait()
            return carry
        lax.fori_loop(0, tm, wait, 0)


def moe_experts(x, g, perm, tile_expert, n_valid, wg, wu, wd, *, tm, tf=512):
    m, d = x.shape
    f = wg.shape[2]
    tf = _tile(f, tf)
    n_tiles, n_f = tile_expert.shape[0], f // tf
    fidx = lambda i, j, nv: jnp.where(i < nv[0], j, n_f - 1)
    grid_spec = pltpu.PrefetchScalarGridSpec(
        num_scalar_prefetch=3,
        grid=(n_tiles, n_f),
        in_specs=[pl.BlockSpec(memory_space=pl.ANY),
                  pl.BlockSpec((1, d), lambda i, j, te, nv, pm: (0, 0)),
                  pl.BlockSpec((None, d, tf), lambda i, j, te, nv, pm: (te[i], 0, fidx(i, j, nv))),
                  pl.BlockSpec((None, d, tf), lambda i, j, te, nv, pm: (te[i], 0, fidx(i, j, nv))),
                  pl.BlockSpec((None, tf, d), lambda i, j, te, nv, pm: (te[i], fidx(i, j, nv), 0))],
        out_specs=pl.BlockSpec(memory_space=pl.ANY),
        scratch_shapes=[pltpu.VMEM((tm, d), F32), pltpu.VMEM((tm, d), BF16), pltpu.VMEM((tm, d), F32),
                        pltpu.SemaphoreType.DMA(()), pltpu.SemaphoreType.DMA(())])
    return pl.pallas_call(
        functools.partial(_moe_group_kernel, tm=tm),
        grid_spec=grid_spec,
        out_shape=jax.ShapeDtypeStruct((2 * m, d), F32),
        compiler_params=_cparams(("arbitrary", "arbitrary")),
    )(tile_expert, n_valid, perm, x, g.reshape(1, d), wg, wu, wd)


def _moe_combine_kernel(x_ref, y_ref, gates_ref, o_ref, *, d):
    gates = gates_ref[...]
    o_ref[...] = x_ref[...] + gates[:, 0:1] * y_ref[:, :d] + gates[:, 1:2] * y_ref[:, d:]


def moe_combine(x, y2, gates, *, tm=1024):
    m, d = x.shape
    tm = _tile(m, tm)
    return pl.pallas_call(
        functools.partial(_moe_combine_kernel, d=d),
        grid=(m // tm,),
        in_specs=[pl.BlockSpec((tm, d), lambda i: (i, 0)),
                  pl.BlockSpec((tm, 2 * d), lambda i: (i, 0)),
                  pl.BlockSpec((tm, LANES), lambda i: (i, 0))],
        out_specs=pl.BlockSpec((tm, d), lambda i: (i, 0)),
        out_shape=jax.ShapeDtypeStruct((m, d), F32),
        compiler_params=_cparams(("parallel",)),
    )(x, y2.reshape(m, 2 * d), gates)


def moe(x, g, router_p, wg, wu, wd, *, tm=1024):
    m = x.shape[0]
    tm = _tile(m, tm)
    n_tiles = (2 * m) // tm + N_EXPERTS
    ids, gates = moe_router(x, g, router_p)
    perm, tile_expert, n_valid = _route_tables(ids[:, :2], tm, n_tiles)
    y2 = moe_experts(x, g, perm, tile_expert, n_valid, wg, wu, wd, tm=tm)
    return moe_combine(x, y2, gates)


def _block_diag(blocks):
    n, h, _ = blocks.shape
    eye = jnp.eye(n, dtype=blocks.dtype)
    return (eye[:, None, :, None] * blocks[:, :, None, :]).reshape(n * h, n * h)


def _seg_ones(n):
    seg = jnp.arange(n) // HEAD
    return (seg[:, None] == seg[None, :]).astype(BF16)


def _even_layer(x, batch, seq, ln_mix, w_in, conv_w, conv_b, gate_a_w, gate_a_b, gate_x_w, gate_x_b,
                lru_lambda, shift_mu, w0, w2, a0, a2, g2, k_k, k_a, r_k, gn_w, gn_b, w_out,
                ln_ffn, ffn_gate, ffn_up, ffn_down):
    lru_w = lru_lambda.shape[0]
    rw_w = w0.shape[0]
    dl, al, gl = w2.shape[0], a2.shape[0], g2.shape[0]
    n_in = w_in.shape[1]
    proj = norm_matmul(x, ln_mix, w_in.astype(BF16), tn=n_in // 2)

    y_lru = lru_branch(proj, batch, seq, conv_w, conv_b,
                       _block_diag(gate_a_w).astype(BF16), gate_a_b,
                       _block_diag(gate_x_w).astype(BF16), gate_x_b, lru_lambda)

    lora = dl + al + gl
    zeros = lambda n: jnp.zeros((n, rw_w), F32)
    w2p = jnp.concatenate([w2, zeros(al + gl)], axis=0).astype(BF16)
    a2p = jnp.concatenate([zeros(dl), a2, zeros(gl)], axis=0).astype(BF16)
    g2p = jnp.concatenate([zeros(dl + al), g2], axis=0).astype(BF16)
    assert lora == w2p.shape[0]
    r, k, v, a, b, lw, g = rwkv_prep(proj, seq, 2 * lru_w, shift_mu, w0, w2p, a0, a2p, g2p, k_k, k_a,
                                     _seg_ones(rw_w))
    y_rwkv = rwkv_scan(r, k, v, a, b, lw, g, batch, seq, r_k.reshape(-1), gn_w, gn_b, _seg_ones(LANES))

    w_out = w_out.astype(BF16)
    x = matmul_residual([y_lru, y_rwkv], [w_out[:lru_w], w_out[lru_w:]], x)
    return ffn(x, ln_ffn, ffn_gate.astype(BF16), ffn_up.astype(BF16), ffn_down.astype(BF16))


def _odd_layer(x, pos, batch, seq, layer_idx, ln_mix, w_qkv, q_norm, k_norm, lq1, lk1, lq2, lk2, subln,
               w_o, ln_ffn, router, moe_gate, moe_up, moe_down):
    d = x.shape[1]
    n_heads = d // (2 * HEAD)
    qd = n_heads * 2 * HEAD
    lambda_init = 0.8 - 0.6 * math.exp(-0.3 * layer_idx)
    reps = qd // HEAD
    q_gain = q_norm * (HEAD ** -0.5 * LOG2E)
    logit_bound = 1.02 * HEAD * jnp.max(jnp.abs(q_gain)) * jnp.max(jnp.abs(k_norm))
    fast = (logit_bound <= MAX_EXP2_ARG).astype(jnp.int32).reshape(1)
    gains = jnp.concatenate([jnp.tile(q_gain, reps),
                             jnp.tile(k_norm, reps),
                             jnp.ones((w_qkv.shape[1] - 2 * qd,), F32)])[None, :]
    seg = jnp.arange(LANES) % HEAD
    inv_freq = ROPE_THETA ** (-(2.0 * (seg % (ROPE_DIM // 2))).astype(F32) / ROPE_DIM)
    freq = jnp.where(seg < ROPE_DIM, inv_freq, 0.0)[None, :].astype(F32)
    qkv = qkv_project(x, ln_mix, w_qkv.astype(BF16), pos, gains, freq, _seg_ones(LANES), 2 * qd)
    lam_params = jnp.stack([lq1, lk1, lq2, lk2]).astype(F32)
    o = diff_attention(qkv, fast, batch, seq, n_heads, lam_params, subln, lambda_init)
    x = matmul_residual([o], [w_o.astype(BF16)], x)
    router_p = jnp.pad(router, ((0, 0), (0, LANES - router.shape[1])))
    return moe(x, ln_ffn, router_p, moe_gate.astype(BF16), moe_up.astype(BF16), moe_down.astype(BF16))


def kernel(x, positions, e_ln_mix, e_w_in, e_conv_w, e_conv_b, e_gate_a_w, e_gate_a_b, e_gate_x_w, e_gate_x_b, e_lru_lambda, e_shift_mu, e_w0, e_w2, e_a0, e_a2, e_g2, e_k_k, e_k_a, e_r_k, e_gn_w, e_gn_b, e_w_out, e_ln_ffn, e_ffn_gate, e_ffn_up, e_ffn_down, o_ln_mix, o_w_qkv, o_q_norm, o_k_norm, o_lambda_q1, o_lambda_k1, o_lambda_q2, o_lambda_k2, o_subln, o_w_o, o_ln_ffn, o_router, o_moe_gate, o_moe_up, o_moe_down):
    batch, seq, d = x.shape
    depth = e_ln_mix.shape[0] + o_ln_mix.shape[0]
    xf = x.reshape(batch * seq, d)
    pos = positions.reshape(batch * seq, 1).astype(F32)
    for i in range(depth):
        j = i // 2
        if i % 2 == 0:
            xf = _even_layer(xf, batch, seq, e_ln_mix[j], e_w_in[j], e_conv_w[j], e_conv_b[j],
                             e_gate_a_w[j], e_gate_a_b[j], e_gate_x_w[j], e_gate_x_b[j],
                             e_lru_lambda[j], e_shift_mu[j], e_w0[j], e_w2[j], e_a0[j], e_a2[j],
                             e_g2[j], e_k_k[j], e_k_a[j], e_r_k[j], e_gn_w[j], e_gn_b[j], e_w_out[j],
                             e_ln_ffn[j], e_ffn_gate[j], e_ffn_up[j], e_ffn_down[j])
        else:
            xf = _odd_layer(xf, pos, batch, seq, i, o_ln_mix[j], o_w_qkv[j], o_q_norm[j], o_k_norm[j],
                            o_lambda_q1[j], o_lambda_k1[j], o_lambda_q2[j], o_lambda_k2[j], o_subln[j],
                            o_w_o[j], o_ln_ffn[j], o_router[j], o_moe_gate[j], o_moe_up[j],
                            o_moe_down[j])
    return xf.reshape(batch, seq, d)
```

```python
import functools
import math

import jax
import jax.numpy as jnp
from jax import lax
from jax.experimental import pallas as pl
from jax.experimental.pallas import tpu as pltpu

F32 = jnp.float32
BF16 = jnp.bfloat16
HIGHEST = lax.Precision.HIGHEST

LANES = 128
SUBLANES = 8
VMEM_LIMIT = 56 * 1024 * 1024

HEAD = 64
CONV_WIDTH = 4
LRU_C = 8.0
GN_EPS = 64e-5
RMS_EPS = 1e-6
ROPE_DIM = 16
ROPE_THETA = 500000.0
N_EXPERTS = 8
CHUNK = 64
NEG_BIG = -1e30
LOG2E = 1.4426950408889634
MAX_EXP2_ARG = 60.0


def _cparams(sem):
    return pltpu.CompilerParams(dimension_semantics=sem, vmem_limit_bytes=VMEM_LIMIT)


def _tile(n, pref):
    t = min(n, pref)
    assert n % t == 0, (n, pref)
    return t


def _nt(a, b, **kw):
    return lax.dot_general(a, b, (((1,), (1,)), ((), ())), preferred_element_type=F32, **kw)


def _tn(a, b, **kw):
    return lax.dot_general(a, b, (((0,), (0,)), ((), ())), preferred_element_type=F32, **kw)


def _dot(a, b, **kw):
    return jnp.dot(a, b, preferred_element_type=F32, **kw)


def _segsum(x, bd):
    hi = x.astype(BF16)
    lo = (x - hi.astype(F32)).astype(BF16)
    return _dot(hi, bd) + _dot(lo, bd)


def _rms(x, g):
    ms = jnp.mean(x * x, axis=-1, keepdims=True)
    return x * lax.rsqrt(ms + RMS_EPS) * g


def _sigmoid(x):
    return 1.0 / (1.0 + jnp.exp(-x))


def _softplus(x):
    return jnp.maximum(x, 0.0) + jnp.log1p(jnp.exp(-jnp.abs(x)))


def _norm_mm_kernel(x_ref, g_ref, w_ref, o_ref, h_ref):
    @pl.when(pl.program_id(1) == 0)
    def _():
        h_ref[...] = _rms(x_ref[...], g_ref[...]).astype(BF16)

    o_ref[...] = _dot(h_ref[...], w_ref[...]).astype(o_ref.dtype)


def norm_matmul(x, g, w, *, tm=1024, tn=512, out_dtype=F32):
    m, d = x.shape
    n = w.shape[1]
    tm, tn = _tile(m, tm), _tile(n, tn)
    return pl.pallas_call(
        _norm_mm_kernel,
        grid=(m // tm, n // tn),
        in_specs=[pl.BlockSpec((tm, d), lambda i, j: (i, 0)),
                  pl.BlockSpec((1, d), lambda i, j: (0, 0)),
                  pl.BlockSpec((d, tn), lambda i, j: (0, j))],
        out_specs=pl.BlockSpec((tm, tn), lambda i, j: (i, j)),
        out_shape=jax.ShapeDtypeStruct((m, n), out_dtype),
        scratch_shapes=[pltpu.VMEM((tm, d), BF16)],
        compiler_params=_cparams(("parallel", "arbitrary")),
    )(x, g.reshape(1, d), w)


def _mm_res_kernel(*refs, n_in):
    ys, ws = refs[:n_in], refs[n_in:2 * n_in]
    res_ref, o_ref = refs[2 * n_in], refs[2 * n_in + 1]
    acc = res_ref[...]
    for y_ref, w_ref in zip(ys, ws):
        acc = acc + _dot(y_ref[...], w_ref[...])
    o_ref[...] = acc


def matmul_residual(ys, ws, res, *, tm=1024, tn=512):
    m, n = res.shape
    tm, tn = _tile(m, tm), _tile(n, tn)
    n_in = len(ys)
    in_specs = [pl.BlockSpec((tm, y.shape[1]), lambda i, j: (i, 0)) for y in ys]
    in_specs += [pl.BlockSpec((w.shape[0], tn), lambda i, j: (0, j)) for w in ws]
    in_specs += [pl.BlockSpec((tm, tn), lambda i, j: (i, j))]
    return pl.pallas_call(
        functools.partial(_mm_res_kernel, n_in=n_in),
        grid=(m // tm, n // tn),
        in_specs=in_specs,
        out_specs=pl.BlockSpec((tm, tn), lambda i, j: (i, j)),
        out_shape=jax.ShapeDtypeStruct((m, n), F32),
        compiler_params=_cparams(("parallel", "arbitrary")),
    )(*ys, *ws, res)


def _lru_kernel(x_ref, gate_ref, cw_ref, cb_ref, wa_ref, ba_ref, wx_ref, bx_ref, lam_ref,
                o_ref, tail_ref, h_ref, *, tm):
    @pl.when(pl.program_id(1) == 0)
    def _():
        tail_ref[...] = jnp.zeros_like(tail_ref)
        h_ref[...] = jnp.zeros_like(h_ref)

    x = x_ref[...]
    xx = jnp.concatenate([tail_ref[...], x], axis=0)
    tail_ref[...] = x[tm - 8:, :]
    cw = cw_ref[...]
    xc = cb_ref[...] + cw[CONV_WIDTH - 1:CONV_WIDTH, :] * x
    for s in range(1, CONV_WIDTH):
        xc = xc + cw[CONV_WIDTH - 1 - s:CONV_WIDTH - s, :] * pltpu.roll(xx, s, axis=0)[8:, :]

    xb = xc.astype(BF16)
    gate_a = _dot(xb, wa_ref[...]) + ba_ref[...]
    gate_x = _dot(xb, wx_ref[...]) + bx_ref[...]
    log_a = -LRU_C * _sigmoid(gate_a) * _softplus(-lam_ref[...])
    a = jnp.exp(log_a)
    u = jnp.sqrt(1.0 - a * a) * _sigmoid(gate_x) * xc

    row = lax.broadcasted_iota(jnp.int32, a.shape, 0)
    s = 1
    while s < tm:
        keep = row >= s
        a_s = jnp.where(keep, pltpu.roll(a, s, axis=0), 1.0)
        u_s = jnp.where(keep, pltpu.roll(u, s, axis=0), 0.0)
        u = a * u_s + u
        a = a * a_s
        s *= 2
    h = a * h_ref[...] + u
    h_ref[...] = h[tm - 1:tm, :]
    o_ref[...] = (jax.nn.gelu(gate_ref[...]) * h).astype(o_ref.dtype)


def lru_branch(proj, batch, seq, conv_w, conv_b, wa_bd, ba, wx_bd, bx, lam, *, tm=256):
    m = proj.shape[0]
    w = lam.shape[0]
    tm = _tile(seq, tm)
    nt = seq // tm
    vec = lambda: pl.BlockSpec((1, w), lambda b, i: (0, 0))
    return pl.pallas_call(
        functools.partial(_lru_kernel, tm=tm),
        grid=(batch, nt),
        in_specs=[pl.BlockSpec((tm, w), lambda b, i: (b * nt + i, 0)),
                  pl.BlockSpec((tm, w), lambda b, i: (b * nt + i, 1)),
                  pl.BlockSpec((CONV_WIDTH, w), lambda b, i: (0, 0)),
                  vec(),
                  pl.BlockSpec((w, w), lambda b, i: (0, 0)), vec(),
                  pl.BlockSpec((w, w), lambda b, i: (0, 0)), vec(),
                  vec()],
        out_specs=pl.BlockSpec((tm, w), lambda b, i: (b * nt + i, 0)),
        out_shape=jax.ShapeDtypeStruct((m, w), BF16),
        scratch_shapes=[pltpu.VMEM((8, w), F32), pltpu.VMEM((1, w), F32)],
        compiler_params=_cparams(("parallel", "arbitrary")),
    )(proj, proj, conv_w, conv_b.reshape(1, w), wa_bd, ba.reshape(1, w), wx_bd, bx.reshape(1, w),
      lam.reshape(1, w))


def _rwkv_prep_kernel(pr_ref, pk_ref, pv_ref, pl_ref, qr_ref, qk_ref, qv_ref, ql_ref,
                      mur_ref, muk_ref, muv_ref, mul_ref, w0_ref, w2_ref, a0_ref, a2_ref, g2_ref,
                      kk_ref, ka_ref, bd_ref,
                      r_out, k_out, v_out, a_out, b_out, lw_out, g_out, *, tm, seq):
    first = (pl.program_id(0) * tm) % seq == 0

    def shift_mix(p_ref, q_ref, mu_ref):
        x = p_ref[...]
        prev = jnp.where(first, 0.0, q_ref[7:8, :])
        row = lax.broadcasted_iota(jnp.int32, x.shape, 0)
        xs = jnp.where(row == 0, prev, pltpu.roll(x, 1, axis=0))
        return x + (xs - x) * mu_ref[...]

    r = shift_mix(pr_ref, qr_ref, mur_ref)
    k = shift_mix(pk_ref, qk_ref, muk_ref)
    v = shift_mix(pv_ref, qv_ref, muv_ref)
    lo = shift_mix(pl_ref, ql_ref, mul_ref)

    wlog = -_softplus(-(w0_ref[...] + _dot(jnp.tanh(lo).astype(BF16), w2_ref[...]))) - 0.5
    a = _sigmoid(a0_ref[...] + _dot(lo.astype(BF16), a2_ref[...]))
    g = _dot(_sigmoid(lo).astype(BF16), g2_ref[...])

    kk = k * kk_ref[...]
    nrm = jnp.sqrt(_segsum(kk * kk, bd_ref[...]))
    kk = kk / jnp.maximum(nrm, 1e-12)

    r_out[...] = r
    k_out[...] = k * (1.0 + (a - 1.0) * ka_ref[...])
    v_out[...] = v
    a_out[...] = -kk
    b_out[...] = kk * a
    lw_out[...] = -jnp.exp(wlog)
    g_out[...] = g


def rwkv_prep(proj, seq, col0, mu, w0, w2p, a0, a2p, g2p, k_k, k_a, bd, *, tm=512):
    m = proj.shape[0]
    w = w0.shape[0]
    lw = w2p.shape[0]
    tm = _tile(seq, tm)
    cb = col0 // w
    lb = (col0 + 3 * w) // lw
    main = lambda c, width: pl.BlockSpec((tm, width), lambda i: (i, c))
    prev = lambda c, width: pl.BlockSpec((8, width), lambda i: (jnp.maximum(i * (tm // 8) - 1, 0), c))
    vec = lambda width: pl.BlockSpec((1, width), lambda i: (0, 0))
    mat = lambda a: pl.BlockSpec(a.shape, lambda i: (0, 0))
    mu_r, mu_k, mu_v, mu_l = (mu[None, 0:w], mu[None, w:2 * w], mu[None, 2 * w:3 * w], mu[None, 3 * w:])
    outs = [jax.ShapeDtypeStruct((m, w), F32)] * 7
    return pl.pallas_call(
        functools.partial(_rwkv_prep_kernel, tm=tm, seq=seq),
        grid=(m // tm,),
        in_specs=[main(cb, w), main(cb + 1, w), main(cb + 2, w), main(lb, lw),
                  prev(cb, w), prev(cb + 1, w), prev(cb + 2, w), prev(lb, lw),
                  vec(w), vec(w), vec(w), vec(lw),
                  vec(w), mat(w2p), vec(w), mat(a2p), mat(g2p), vec(w), vec(w), mat(bd)],
        out_specs=[pl.BlockSpec((tm, w), lambda i: (i, 0))] * 7,
        out_shape=outs,
        compiler_params=_cparams(("parallel",)),
    )(proj, proj, proj, proj, proj, proj, proj, proj, mu_r, mu_k, mu_v, mu_l,
      w0.reshape(1, w), w2p, a0.reshape(1, w), a2p, g2p, k_k.reshape(1, w), k_a.reshape(1, w), bd)


def _rwkv_scan_kernel(r_ref, k_ref, v_ref, a_ref, b_ref, lw_ref, g_ref, rk_ref, gnw_ref, gnb_ref,
                      bd_ref, o_ref, s_ref, *, batch, width):
    @pl.when(pl.program_id(0) == 0)
    def _():
        s_ref[...] = jnp.zeros_like(s_ref)

    c = CHUNK
    tri = (lax.broadcasted_iota(jnp.int32, (c, c), 0) >= lax.broadcasted_iota(jnp.int32, (c, c), 1))
    lane = lax.broadcasted_iota(jnp.int32, (c, LANES), 1)
    head0 = lane < HEAD
    i2 = lax.broadcasted_iota(jnp.int32, (2 * c, 2 * c), 0)
    j2 = lax.broadcasted_iota(jnp.int32, (2 * c, 2 * c), 1)
    strict = i2 > j2
    incl = i2 >= j2
    bd = bd_ref[...]
    n_steps = int(math.log2(c))

    def stack(x):
        xb = x.astype(BF16)
        zero = jnp.zeros_like(xb)
        return jnp.concatenate([jnp.where(head0, xb, zero), jnp.where(head0, zero, xb)], axis=0)

    chains = []
    for bi in range(batch):
        lw = lw_ref[bi]
        cum = _dot(tri.astype(F32), lw, precision=HIGHEST)
        tot = cum[c - 1:c, :]
        g_out = jnp.exp(-cum)
        g_suf = jnp.exp(tot - cum)
        g_tot = jnp.exp(tot)
        r, k, v, b = r_ref[bi], k_ref[bi], v_ref[bi], b_ref[bi]
        rt = r * jnp.exp(cum)
        kt = k * g_out
        at = a_ref[bi] * jnp.exp(cum - lw)
        bt = b * g_out
        ks = k * g_suf
        bs = b * g_suf
        rkv = r * k * rk_ref[...]
        for p in range(width // LANES):
            sl = slice(p * LANES, (p + 1) * LANES)
            chains.append(dict(
                bi=bi, p=p, sl=sl, v=v[:, sl], rkv=rkv[:, sl], g_tot=g_tot[:, sl],
                v_s=stack(v[:, sl]),
                ar=jnp.concatenate([stack(at[:, sl]), stack(rt[:, sl])], axis=0),
                kb=jnp.concatenate([stack(kt[:, sl]), stack(bt[:, sl])], axis=0),
                suf=jnp.concatenate([stack(ks[:, sl]), stack(bs[:, sl])], axis=0)))

    for ch in chains:
        ch["state"] = s_ref[ch["bi"], ch["p"]]
        ch["gram"] = _nt(ch["ar"], ch["kb"])
        ch["xs"] = _nt(ch["ar"], ch["state"].astype(BF16))
    for ch in chains:
        gram = ch["gram"]
        ak = jnp.where(strict, gram[:2 * c, :2 * c], 0.0)
        rk = jnp.where(incl, gram[2 * c:, :2 * c], 0.0)
        ch["lpow"] = jnp.where(strict, gram[:2 * c, 2 * c:], 0.0)
        ch["rb"] = jnp.where(incl, gram[2 * c:, 2 * c:], 0.0).astype(BF16)
        ch["lv"] = _dot(jnp.concatenate([ak, rk], axis=0).astype(BF16), ch["v_s"])
    for ch in chains:
        ch["u"] = ch["xs"][:2 * c] + ch["lv"][:2 * c]
    for i in range(n_steps):
        for ch in chains:
            lb = ch["lpow"].astype(BF16)
            ch["u"] = ch["u"] + _dot(lb, ch["u"].astype(BF16))
            if i + 1 < n_steps:
                ch["lpow"] = _dot(lb, lb)
    for ch in chains:
        ub = ch["u"].astype(BF16)
        o_s = ch["xs"][2 * c:] + ch["lv"][2 * c:] + _dot(ch["rb"], ub)
        ch["o"] = o_s[:c] + o_s[c:]
        upd = _tn(jnp.concatenate([ch["v_s"], ub], axis=0), ch["suf"])
        s_ref[ch["bi"], ch["p"]] = ch["state"] * ch["g_tot"] + upd
    for ch in chains:
        bi, sl, o = ch["bi"], ch["sl"], ch["o"]
        mean = _segsum(o, bd) * (1.0 / HEAD)
        d = o - mean
        var = _segsum(d * d, bd) * (1.0 / HEAD)
        on = d * lax.rsqrt(var + GN_EPS) * gnw_ref[:, sl] + gnb_ref[:, sl]
        bonus = _segsum(ch["rkv"], bd) * ch["v"]
        o_ref[bi, :, sl] = ((on + bonus) * g_ref[bi, :, sl]).astype(o_ref.dtype)


def rwkv_scan(r, k, v, a, b, lw, g, batch, seq, r_k, gn_w, gn_b, bd128):
    m, w = r.shape
    nc = seq // CHUNK
    blk = lambda: pl.BlockSpec((batch, CHUNK, w), lambda ci: (0, ci, 0))
    vec = lambda: pl.BlockSpec((1, w), lambda ci: (0, 0))
    as3d = lambda x: x.reshape(batch, seq, w)
    out = pl.pallas_call(
        functools.partial(_rwkv_scan_kernel, batch=batch, width=w),
        grid=(nc,),
        in_specs=[blk() for _ in range(7)] + [vec(), vec(), vec(),
                                               pl.BlockSpec((LANES, LANES), lambda ci: (0, 0))],
        out_specs=blk(),
        out_shape=jax.ShapeDtypeStruct((batch, seq, w), BF16),
        scratch_shapes=[pltpu.VMEM((batch, w // LANES, LANES, LANES), F32)],
        compiler_params=_cparams(("arbitrary",)),
    )(as3d(r), as3d(k), as3d(v), as3d(a), as3d(b), as3d(lw), as3d(g),
      r_k.reshape(1, w), gn_w.reshape(1, w), gn_b.reshape(1, w), bd128)
    return out.reshape(m, w)


def _ffn_kernel(x_ref, g_ref, wg_ref, wu_ref, wd_ref, o_ref, h_ref, acc_ref):
    f = pl.program_id(1)

    @pl.when(f == 0)
    def _():
        h_ref[...] = _rms(x_ref[...], g_ref[...]).astype(BF16)
        acc_ref[...] = x_ref[...]

    h = h_ref[...]
    gate = _dot(h, wg_ref[...])
    up = _dot(h, wu_ref[...])
    act = (gate * _sigmoid(gate) * up).astype(BF16)
    acc_ref[...] += _dot(act, wd_ref[...])

    @pl.when(f == pl.num_programs(1) - 1)
    def _():
        o_ref[...] = acc_ref[...]


def ffn(x, g, wg, wu, wd, *, tm=1024, tf=256):
    m, d = x.shape
    f = wg.shape[1]
    tm, tf = _tile(m, tm), _tile(f, tf)
    return pl.pallas_call(
        _ffn_kernel,
        grid=(m // tm, f // tf),
        in_specs=[pl.BlockSpec((tm, d), lambda i, j: (i, 0)),
                  pl.BlockSpec((1, d), lambda i, j: (0, 0)),
                  pl.BlockSpec((d, tf), lambda i, j: (0, j)),
                  pl.BlockSpec((d, tf), lambda i, j: (0, j)),
                  pl.BlockSpec((tf, d), lambda i, j: (j, 0))],
        out_specs=pl.BlockSpec((tm, d), lambda i, j: (i, 0)),
        out_shape=jax.ShapeDtypeStruct((m, d), F32),
        scratch_shapes=[pltpu.VMEM((tm, d), BF16), pltpu.VMEM((tm, d), F32)],
        compiler_params=_cparams(("parallel", "arbitrary")),
    )(x, g.reshape(1, d), wg, wu, wd)


def _qkv_kernel(x_ref, g_ref, w_ref, pos_ref, gain_ref, freq_ref, bd_ref, o_ref,
                h_ref, cos_ref, sina_ref, sinb_ref, *, n_rot_tiles, tn):
    j = pl.program_id(1)

    @pl.when(j == 0)
    def _():
        h_ref[...] = _rms(x_ref[...], g_ref[...]).astype(BF16)
        ang = pos_ref[...] * freq_ref[...]
        seg = lax.broadcasted_iota(jnp.int32, ang.shape, 1) % HEAD
        half = ROPE_DIM // 2
        cos_ref[...] = jnp.where(seg < ROPE_DIM, jnp.cos(ang), 1.0)
        sin = jnp.sin(ang)
        sina_ref[...] = jnp.where(seg < half, -sin, 0.0)
        sinb_ref[...] = jnp.where((seg >= half) & (seg < ROPE_DIM), sin, 0.0)

    y = _dot(h_ref[...], w_ref[...])

    @pl.when(j < n_rot_tiles)
    def _():
        half = ROPE_DIM // 2
        for c in range(tn // LANES):
            sl = slice(c * LANES, (c + 1) * LANES)
            yc = y[:, sl]
            ms = _segsum(yc * yc, bd_ref[...]) * (1.0 / HEAD)
            yn = yc * lax.rsqrt(ms + RMS_EPS) * gain_ref[:, sl]
            rot = (yn * cos_ref[...] + pltpu.roll(yn, LANES - half, axis=1) * sina_ref[...]
                   + pltpu.roll(yn, half, axis=1) * sinb_ref[...])
            o_ref[:, sl] = rot.astype(o_ref.dtype)

    @pl.when(j >= n_rot_tiles)
    def _():
        o_ref[...] = y.astype(o_ref.dtype)


def qkv_project(x, g, w, pos, gains, freq, bd128, n_rot_cols, *, tm=1024, tn=512):
    m, d = x.shape
    n = w.shape[1]
    tm, tn = _tile(m, tm), _tile(n, tn)
    assert n_rot_cols % tn == 0
    return pl.pallas_call(
        functools.partial(_qkv_kernel, n_rot_tiles=n_rot_cols // tn, tn=tn),
        grid=(m // tm, n // tn),
        in_specs=[pl.BlockSpec((tm, d), lambda i, j: (i, 0)),
                  pl.BlockSpec((1, d), lambda i, j: (0, 0)),
                  pl.BlockSpec((d, tn), lambda i, j: (0, j)),
                  pl.BlockSpec((tm, 1), lambda i, j: (i, 0)),
                  pl.BlockSpec((1, tn), lambda i, j: (0, j)),
                  pl.BlockSpec((1, LANES), lambda i, j: (0, 0)),
                  pl.BlockSpec((LANES, LANES), lambda i, j: (0, 0))],
        out_specs=pl.BlockSpec((tm, tn), lambda i, j: (i, j)),
        out_shape=jax.ShapeDtypeStruct((m, n), BF16),
        scratch_shapes=[pltpu.VMEM((tm, d), BF16), pltpu.VMEM((tm, LANES), F32),
                        pltpu.VMEM((tm, LANES), F32), pltpu.VMEM((tm, LANES), F32)],
        compiler_params=_cparams(("parallel", "arbitrary")),
    )(x, g.reshape(1, d), w, pos, gains, freq, bd128)


def _attn_kernel(fast_ref, q_ref, k_ref, v_ref, lam_ref, subln_ref, o_ref, m_ref, ls_ref, l_ref, acc_ref,
                 *, tq, lambda_init):
    qi = pl.program_id(2)
    q = q_ref[...]
    lane = lax.broadcasted_iota(jnp.int32, q.shape, 1)
    zero = jnp.zeros_like(q)
    qs = jnp.concatenate([jnp.where(lane < HEAD, q, zero), jnp.where(lane < HEAD, zero, q)], axis=0)
    acc_ref[...] = jnp.zeros_like(acc_ref)

    def scores(j, masked):
        start = pl.multiple_of(j * tq, tq)
        s = _nt(qs, k_ref[pl.ds(start, tq), :])
        if masked:
            row = lax.broadcasted_iota(jnp.int32, (tq, tq), 0)
            col = lax.broadcasted_iota(jnp.int32, (tq, tq), 1)
            keep = jnp.concatenate([col <= row, col <= row], axis=0)
            s = jnp.where(keep, s, NEG_BIG)
        return s, v_ref[pl.ds(start, tq), :]

    def sweep(step):
        def body(j, carry):
            step(j, False)
            return carry
        lax.fori_loop(0, qi, body, 0)
        step(qi, True)

    @pl.when(fast_ref[0] == 1)
    def _():
        l_ref[...] = jnp.zeros_like(l_ref)

        def step(j, masked):
            s, vb = scores(j, masked)
            p = jnp.exp2(s)
            part = p[:, 0:LANES]
            for c in range(1, tq // LANES):
                part = part + p[:, c * LANES:(c + 1) * LANES]
            l_ref[...] += part
            acc_ref[...] += _dot(p.astype(BF16), vb)

        sweep(step)
        ls_ref[...] = jnp.sum(l_ref[...], axis=-1, keepdims=True)

    @pl.when(fast_ref[0] == 0)
    def _():
        m_ref[...] = jnp.full_like(m_ref, NEG_BIG)
        ls_ref[...] = jnp.zeros_like(ls_ref)

        def step(j, masked):
            s, vb = scores(j, masked)
            m_old = m_ref[...]
            m_new = jnp.maximum(m_old, jnp.max(s, axis=-1, keepdims=True))
            alpha = jnp.exp2(m_old - m_new)
            p = jnp.exp2(s - m_new)
            ls_ref[...] = alpha * ls_ref[...] + jnp.sum(p, axis=-1, keepdims=True)
            acc_ref[...] = alpha * acc_ref[...] + _dot(p.astype(BF16), vb)
            m_ref[...] = m_new

        sweep(step)

    lq = lam_ref[...]
    lam = (jnp.exp(jnp.sum(lq[0:1] * lq[1:2], axis=-1, keepdims=True))
           - jnp.exp(jnp.sum(lq[2:3] * lq[3:4], axis=-1, keepdims=True)) + lambda_init)
    o = acc_ref[...] / ls_ref[...]
    o = o[:tq] - lam * o[tq:]
    o = _rms(o, subln_ref[...]) * (1.0 - lambda_init)
    o_ref[...] = o.astype(o_ref.dtype)


def diff_attention(qkv, fast, batch, seq, n_heads, lam_params, subln, lambda_init, *, tq=512):
    m = qkv.shape[0]
    tq = _tile(seq, tq)
    nq = seq // tq
    grid_spec = pltpu.PrefetchScalarGridSpec(
        num_scalar_prefetch=1,
        grid=(batch, n_heads, nq),
        in_specs=[pl.BlockSpec((tq, LANES), lambda b, h, i, f: (b * nq + i, h)),
                  pl.BlockSpec((seq, LANES), lambda b, h, i, f: (b, n_heads + h)),
                  pl.BlockSpec((seq, LANES), lambda b, h, i, f: (b, 2 * n_heads + h)),
                  pl.BlockSpec((4, HEAD), lambda b, h, i, f: (0, 0)),
                  pl.BlockSpec((1, LANES), lambda b, h, i, f: (0, 0))],
        out_specs=pl.BlockSpec((tq, LANES), lambda b, h, i, f: (b * nq + i, h)),
        scratch_shapes=[pltpu.VMEM((2 * tq, 1), F32), pltpu.VMEM((2 * tq, 1), F32),
                        pltpu.VMEM((2 * tq, LANES), F32), pltpu.VMEM((2 * tq, LANES), F32)])
    return pl.pallas_call(
        functools.partial(_attn_kernel, tq=tq, lambda_init=lambda_init),
        grid_spec=grid_spec,
        out_shape=jax.ShapeDtypeStruct((m, n_heads * LANES), BF16),
        compiler_params=_cparams(("parallel", "parallel", "arbitrary")),
    )(fast, qkv, qkv, qkv, lam_params, subln.reshape(1, LANES))


def _router_kernel(x_ref, g_ref, router_ref, ids_ref, gates_ref):
    h = _rms(x_ref[...], g_ref[...])
    logits = _dot(h, router_ref[...], precision=HIGHEST)
    lane = lax.broadcasted_iota(jnp.int32, logits.shape, 1)
    logits = jnp.where(lane < N_EXPERTS, logits, NEG_BIG)
    v1 = jnp.max(logits, axis=-1, keepdims=True)
    i1 = jnp.min(jnp.where(logits == v1, lane, LANES), axis=-1, keepdims=True)
    rest = jnp.where(lane == i1, NEG_BIG, logits)
    v2 = jnp.max(rest, axis=-1, keepdims=True)
    i2 = jnp.min(jnp.where(rest == v2, lane, LANES), axis=-1, keepdims=True)
    e2 = jnp.exp(v2 - v1)
    ids_ref[...] = jnp.where(lane == 0, i1, i2)
    gates_ref[...] = jnp.where(lane == 0, 1.0 / (1.0 + e2), e2 / (1.0 + e2))


def moe_router(x, g, router_p, *, tm=1024):
    m, d = x.shape
    tm = _tile(m, tm)
    return pl.pallas_call(
        _router_kernel,
        grid=(m // tm,),
        in_specs=[pl.BlockSpec((tm, d), lambda i: (i, 0)),
                  pl.BlockSpec((1, d), lambda i: (0, 0)),
                  pl.BlockSpec((d, LANES), lambda i: (0, 0))],
        out_specs=[pl.BlockSpec((tm, LANES), lambda i: (i, 0))] * 2,
        out_shape=[jax.ShapeDtypeStruct((m, LANES), jnp.int32), jax.ShapeDtypeStruct((m, LANES), F32)],
        compiler_params=_cparams(("parallel",)),
    )(x, g.reshape(1, d), router_p)


def _route_tables(ids, tm, n_tiles):
    n_pairs = ids.shape[0] * 2
    n_rows = n_tiles * tm
    shift = max(n_pairs, n_rows).bit_length()
    e_flat = ids.reshape(-1)
    experts = jnp.arange(N_EXPERTS, dtype=jnp.int32)
    counts = jnp.sum((e_flat[:, None] == experts[None, :]).astype(jnp.int32), axis=0)
    padded = ((counts + tm - 1) // tm) * tm
    ends = jnp.cumsum(padded)
    pad_ends = jnp.cumsum(padded - counts)
    q = jnp.arange(n_rows - n_pairs, dtype=jnp.int32)
    pad_expert = jnp.sum((q[:, None] >= pad_ends[None, :]).astype(jnp.int32), axis=1)
    low = (1 << shift) - 1
    keys = jnp.concatenate([(e_flat << shift) | jnp.arange(n_pairs, dtype=jnp.int32),
                            (pad_expert << shift) | low])
    keys = jnp.sort(keys)
    perm = jnp.where((keys & low) == low, -1, keys & low)
    starts = jnp.arange(n_tiles, dtype=jnp.int32) * tm
    tile_expert = jnp.sum((starts[:, None] >= ends[None, :]).astype(jnp.int32), axis=1)
    n_valid = ends[-1] // tm
    last_expert = jnp.sum(jnp.where(jnp.arange(n_tiles) == n_valid - 1, tile_expert, 0))
    tile_expert = jnp.where(jnp.arange(n_tiles) < n_valid, tile_expert, last_expert)
    n_real = jnp.sum((perm >= 0).reshape(n_tiles, tm).astype(jnp.int32), axis=1)
    return perm, tile_expert.astype(jnp.int32), n_valid.reshape(1).astype(jnp.int32), n_real


def _moe_group_kernel(te_ref, nv_ref, perm_ref, nr_ref, x_hbm, g_ref, wg_ref, wu_ref, wd_ref, y_hbm,
                      xbuf, ybuf, h_ref, acc_ref, gsem, ssem, *, tm, n_tokens):
    i = pl.program_id(0)
    f = pl.program_id(1)
    valid = i < nv_ref[0]
    base = i * tm
    d = h_ref.shape[1]
    n_chunks = d // LANES

    def tile_rows(k):
        return pl.ds(pl.multiple_of(k * SUBLANES, SUBLANES), SUBLANES)

    def tile_in(r, t):
        return pltpu.make_async_copy(x_hbm.at[tile_rows(t), :], xbuf.at[tile_rows(r), :], gsem)

    def tile_out(r, dst):
        return pltpu.make_async_copy(ybuf.at[tile_rows(r), :], y_hbm.at[tile_rows(dst), :], ssem)

    def chunk_rows(c):
        return pl.ds(c, tm, stride=SUBLANES)

    @pl.when(valid & (f == 0))
    def _():
        def start(r, carry):
            j = perm_ref[base + r]
            tile_in(r, jnp.maximum(j, 0) >> 1).start()
            return carry
        lax.fori_loop(0, tm, start, 0, unroll=8)

        def wait(r, carry):
            tile_in(r, 0).wait()
            return carry
        lax.fori_loop(0, tm, wait, 0, unroll=8)

        ss = jnp.zeros((tm, 1), F32)
        for c in range(n_chunks):
            xc = xbuf[chunk_rows(c), :]
            ss = ss + jnp.sum(xc * xc, axis=-1, keepdims=True)
        inv = lax.rsqrt(ss * (1.0 / d) + RMS_EPS)
        for c in range(n_chunks):
            sl = slice(c * LANES, (c + 1) * LANES)
            h_ref[:, sl] = (xbuf[chunk_rows(c), :] * inv * g_ref[:, sl]).astype(BF16)
        acc_ref[...] = jnp.zeros_like(acc_ref)

    @pl.when(valid)
    def _():
        h = h_ref[...]
        gate = _dot(h, wg_ref[...])
        up = _dot(h, wu_ref[...])
        act = (gate * _sigmoid(gate) * up).astype(BF16)
        acc_ref[...] += _dot(act, wd_ref[...])

    @pl.when(valid & (f == pl.num_programs(1) - 1))
    def _():
        for c in range(n_chunks):
            ybuf[chunk_rows(c), :] = acc_ref[:, c * LANES:(c + 1) * LANES]

        n_real = nr_ref[i]

        def start(r):
            j = perm_ref[base + r]
            tile_out(r, (j & 1) * n_tokens + (j >> 1)).start()

        def start8(r8, carry):
            for k in range(SUBLANES):
                start(r8 * SUBLANES + k)
            return carry
        lax.fori_loop(0, n_real // SUBLANES, start8, 0)

        def start1(r, carry):
            start(r)
            return carry
        lax.fori_loop((n_real // SUBLANES) * SUBLANES, n_real, start1, 0)

        def wait(r, carry):
            tile_out(r, 0).wait()
            return carry
        lax.fori_loop(0, n_real, wait, 0)


def moe_experts(x8, g, perm, tile_expert, n_valid, n_real, wg, wu, wd, *, tm, tf=512):
    d = g.shape[0]
    m = x8.shape[0] // SUBLANES
    f = wg.shape[2]
    tf = _tile(f, tf)
    n_tiles, n_f = tile_expert.shape[0], f // tf
    fidx = lambda i, j, nv: jnp.where(i < nv[0], j, n_f - 1)
    grid_spec = pltpu.PrefetchScalarGridSpec(
        num_scalar_prefetch=4,
        grid=(n_tiles, n_f),
        in_specs=[pl.BlockSpec(memory_space=pl.ANY),
                  pl.BlockSpec((1, d), lambda i, j, te, nv, pm, nr: (0, 0)),
                  pl.BlockSpec((None, d, tf), lambda i, j, te, nv, pm, nr: (te[i], 0, fidx(i, j, nv))),
                  pl.BlockSpec((None, d, tf), lambda i, j, te, nv, pm, nr: (te[i], 0, fidx(i, j, nv))),
                  pl.BlockSpec((None, tf, d), lambda i, j, te, nv, pm, nr: (te[i], fidx(i, j, nv), 0))],
        out_specs=pl.BlockSpec(memory_space=pl.ANY),
        scratch_shapes=[pltpu.VMEM((tm * SUBLANES, LANES), F32), pltpu.VMEM((tm * SUBLANES, LANES), F32),
                        pltpu.VMEM((tm, d), BF16), pltpu.VMEM((tm, d), F32),
                        pltpu.SemaphoreType.DMA(()), pltpu.SemaphoreType.DMA(())])
    return pl.pallas_call(
        functools.partial(_moe_group_kernel, tm=tm, n_tokens=m),
        grid_spec=grid_spec,
        out_shape=jax.ShapeDtypeStruct((2 * m * SUBLANES, LANES), F32),
        compiler_params=_cparams(("arbitrary", "arbitrary")),
    )(tile_expert, n_valid, perm, n_real, x8, g.reshape(1, d), wg, wu, wd)


def _moe_combine_kernel(x_ref, y0_ref, y1_ref, gates_ref, o_ref, *, tm):
    gates = gates_ref[...]
    g0, g1 = gates[:, 0:1], gates[:, 1:2]
    for c in range(x_ref.shape[1] // LANES):
        sl = slice(c * LANES, (c + 1) * LANES)
        rows = pl.ds(c, tm, stride=SUBLANES)
        o_ref[:, sl] = x_ref[:, sl] + g0 * y0_ref[rows, :] + g1 * y1_ref[rows, :]


def moe_combine(x, y8, gates, *, tm=1024):
    m, d = x.shape
    tm = _tile(m, tm)
    nb = m // tm
    return pl.pallas_call(
        functools.partial(_moe_combine_kernel, tm=tm),
        grid=(nb,),
        in_specs=[pl.BlockSpec((tm, d), lambda i: (i, 0)),
                  pl.BlockSpec((tm * SUBLANES, LANES), lambda i: (i, 0)),
                  pl.BlockSpec((tm * SUBLANES, LANES), lambda i: (nb + i, 0)),
                  pl.BlockSpec((tm, LANES), lambda i: (i, 0))],
        out_specs=pl.BlockSpec((tm, d), lambda i: (i, 0)),
        out_shape=jax.ShapeDtypeStruct((m, d), F32),
        compiler_params=_cparams(("parallel",)),
    )(x, y8, y8, gates)


def moe(x, g, router_p, wg, wu, wd, *, tm=1024):
    m, d = x.shape
    assert d == SUBLANES * LANES
    tm = _tile(m, tm)
    n_tiles = (2 * m) // tm + N_EXPERTS
    ids, gates = moe_router(x, g, router_p)
    perm, tile_expert, n_valid, n_real = _route_tables(ids[:, :2], tm, n_tiles)
    x8 = x.reshape(m * SUBLANES, LANES)
    y8 = moe_experts(x8, g, perm, tile_expert, n_valid, n_real, wg, wu, wd, tm=tm)
    return moe_combine(x, y8, gates, tm=tm)


def _block_diag(blocks):
    n, h, _ = blocks.shape
    eye = jnp.eye(n, dtype=blocks.dtype)
    return (eye[:, None, :, None] * blocks[:, :, None, :]).reshape(n * h, n * h)


def _seg_ones(n):
    seg = jnp.arange(n) // HEAD
    return (seg[:, None] == seg[None, :]).astype(BF16)


def _even_layer(x, batch, seq, ln_mix, w_in, conv_w, conv_b, gate_a_w, gate_a_b, gate_x_w, gate_x_b,
                lru_lambda, shift_mu, w0, w2, a0, a2, g2, k_k, k_a, r_k, gn_w, gn_b, w_out,
                ln_ffn, ffn_gate, ffn_up, ffn_down):
    lru_w = lru_lambda.shape[0]
    rw_w = w0.shape[0]
    dl, al, gl = w2.shape[0], a2.shape[0], g2.shape[0]
    n_in = w_in.shape[1]
    proj = norm_matmul(x, ln_mix, w_in.astype(BF16), tn=n_in // 2)

    y_lru = lru_branch(proj, batch, seq, conv_w, conv_b,
                       _block_diag(gate_a_w).astype(BF16), gate_a_b,
                       _block_diag(gate_x_w).astype(BF16), gate_x_b, lru_lambda)

    lora = dl + al + gl
    zeros = lambda n: jnp.zeros((n, rw_w), F32)
    w2p = jnp.concatenate([w2, zeros(al + gl)], axis=0).astype(BF16)
    a2p = jnp.concatenate([zeros(dl), a2, zeros(gl)], axis=0).astype(BF16)
    g2p = jnp.concatenate([zeros(dl + al), g2], axis=0).astype(BF16)
    assert lora == w2p.shape[0]
    r, k, v, a, b, lw, g = rwkv_prep(proj, seq, 2 * lru_w, shift_mu, w0, w2p, a0, a2p, g2p, k_k, k_a,
                                     _seg_ones(rw_w))
    y_rwkv = rwkv_scan(r, k, v, a, b, lw, g, batch, seq, r_k.reshape(-1), gn_w, gn_b, _seg_ones(LANES))

    w_out = w_out.astype(BF16)
    x = matmul_residual([y_lru, y_rwkv], [w_out[:lru_w], w_out[lru_w:]], x)
    return ffn(x, ln_ffn, ffn_gate.astype(BF16), ffn_up.astype(BF16), ffn_down.astype(BF16))


def _odd_layer(x, pos, batch, seq, layer_idx, ln_mix, w_qkv, q_norm, k_norm, lq1, lk1, lq2, lk2, subln,
               w_o, ln_ffn, router, moe_gate, moe_up, moe_down):
    d = x.shape[1]
    n_heads = d // (2 * HEAD)
    qd = n_heads * 2 * HEAD
    lambda_init = 0.8 - 0.6 * math.exp(-0.3 * layer_idx)
    reps = qd // HEAD
    q_gain = q_norm * (HEAD ** -0.5 * LOG2E)
    logit_bound = 1.02 * HEAD * jnp.max(jnp.abs(q_gain)) * jnp.max(jnp.abs(k_norm))
    fast = (logit_bound <= MAX_EXP2_ARG).astype(jnp.int32).reshape(1)
    gains = jnp.concatenate([jnp.tile(q_gain, reps),
                             jnp.tile(k_norm, reps),
                             jnp.ones((w_qkv.shape[1] - 2 * qd,), F32)])[None, :]
    seg = jnp.arange(LANES) % HEAD
    inv_freq = ROPE_THETA ** (-(2.0 * (seg % (ROPE_DIM // 2))).astype(F32) / ROPE_DIM)
    freq = jnp.where(seg < ROPE_DIM, inv_freq, 0.0)[None, :].astype(F32)
    qkv = qkv_project(x, ln_mix, w_qkv.astype(BF16), pos, gains, freq, _seg_ones(LANES), 2 * qd)
    lam_params = jnp.stack([lq1, lk1, lq2, lk2]).astype(F32)
    o = diff_attention(qkv, fast, batch, seq, n_heads, lam_params, subln, lambda_init)
    x = matmul_residual([o], [w_o.astype(BF16)], x)
    router_p = jnp.pad(router, ((0, 0), (0, LANES - router.shape[1])))
    return moe(x, ln_ffn, router_p, moe_gate.astype(BF16), moe_up.astype(BF16), moe_down.astype(BF16))


def kernel(x, positions, e_ln_mix, e_w_in, e_conv_w, e_conv_b, e_gate_a_w, e_gate_a_b, e_gate_x_w, e_gate_x_b, e_lru_lambda, e_shift_mu, e_w0, e_w2, e_a0, e_a2, e_g2, e_k_k, e_k_a, e_r_k, e_gn_w, e_gn_b, e_w_out, e_ln_ffn, e_ffn_gate, e_ffn_up, e_ffn_down, o_ln_mix, o_w_qkv, o_q_norm, o_k_norm, o_lambda_q1, o_lambda_k1, o_lambda_q2, o_lambda_k2, o_subln, o_w_o, o_ln_ffn, o_router, o_moe_gate, o_moe_up, o_moe_down):
    batch, seq, d = x.shape
    depth = e_ln_mix.shape[0] + o_ln_mix.shape[0]
    xf = x.reshape(batch * seq, d)
    pos = positions.reshape(batch * seq, 1).astype(F32)
    for i in range(depth):
        j = i // 2
        if i % 2 == 0:
            xf = _even_layer(xf, batch, seq, e_ln_mix[j], e_w_in[j], e_conv_w[j], e_conv_b[j],
                             e_gate_a_w[j], e_gate_a_b[j], e_gate_x_w[j], e_gate_x_b[j],
                             e_lru_lambda[j], e_shift_mu[j], e_w0[j], e_w2[j], e_a0[j], e_a2[j],
                             e_g2[j], e_k_k[j], e_k_a[j], e_r_k[j], e_gn_w[j], e_gn_b[j], e_w_out[j],
                             e_ln_ffn[j], e_ffn_gate[j], e_ffn_up[j], e_ffn_down[j])
        else:
            xf = _odd_layer(xf, pos, batch, seq, i, o_ln_mix[j], o_w_qkv[j], o_q_norm[j], o_k_norm[j],
                            o_lambda_q1[j], o_lambda_k1[j], o_lambda_q2[j], o_lambda_k2[j], o_subln[j],
                            o_w_o[j], o_ln_ffn[j], o_router[j], o_moe_gate[j], o_moe_up[j],
                            o_moe_down[j])
    return xf.reshape(batch, seq, d)
```

```python
import functools
import math

import jax
import jax.numpy as jnp
from jax import lax
from jax.experimental import pallas as pl
from jax.experimental.pallas import tpu as pltpu

F32 = jnp.float32
BF16 = jnp.bfloat16
HIGHEST = lax.Precision.HIGHEST

LANES = 128
SUBLANES = 8
VMEM_LIMIT = 56 * 1024 * 1024

HEAD = 64
CONV_WIDTH = 4
LRU_C = 8.0
GN_EPS = 64e-5
RMS_EPS = 1e-6
ROPE_DIM = 16
ROPE_THETA = 500000.0
N_EXPERTS = 8
CHUNK = 64
NEG_BIG = -1e30
LOG2E = 1.4426950408889634
MAX_EXP2_ARG = 60.0


def _cparams(sem):
    return pltpu.CompilerParams(dimension_semantics=sem, vmem_limit_bytes=VMEM_LIMIT)


def _tile(n, pref):
    t = min(n, pref)
    assert n % t == 0, (n, pref)
    return t


def _nt(a, b, **kw):
    return lax.dot_general(a, b, (((1,), (1,)), ((), ())), preferred_element_type=F32, **kw)


def _tn(a, b, **kw):
    return lax.dot_general(a, b, (((0,), (0,)), ((), ())), preferred_element_type=F32, **kw)


def _dot(a, b, **kw):
    return jnp.dot(a, b, preferred_element_type=F32, **kw)


def _segsum(x, bd):
    hi = x.astype(BF16)
    lo = (x - hi.astype(F32)).astype(BF16)
    return _dot(hi, bd) + _dot(lo, bd)


def _rms(x, g):
    ms = jnp.mean(x * x, axis=-1, keepdims=True)
    return x * lax.rsqrt(ms + RMS_EPS) * g


def _sigmoid(x):
    return 1.0 / (1.0 + jnp.exp(-x))


def _softplus(x):
    return jnp.maximum(x, 0.0) + jnp.log1p(jnp.exp(-jnp.abs(x)))


def _norm_mm_kernel(x_ref, g_ref, w_ref, o_ref, h_ref):
    @pl.when(pl.program_id(1) == 0)
    def _():
        h_ref[...] = _rms(x_ref[...], g_ref[...]).astype(BF16)

    o_ref[...] = _dot(h_ref[...], w_ref[...]).astype(o_ref.dtype)


def norm_matmul(x, g, w, *, tm=1024, tn=512, out_dtype=F32):
    m, d = x.shape
    n = w.shape[1]
    tm, tn = _tile(m, tm), _tile(n, tn)
    return pl.pallas_call(
        _norm_mm_kernel,
        grid=(m // tm, n // tn),
        in_specs=[pl.BlockSpec((tm, d), lambda i, j: (i, 0)),
                  pl.BlockSpec((1, d), lambda i, j: (0, 0)),
                  pl.BlockSpec((d, tn), lambda i, j: (0, j))],
        out_specs=pl.BlockSpec((tm, tn), lambda i, j: (i, j)),
        out_shape=jax.ShapeDtypeStruct((m, n), out_dtype),
        scratch_shapes=[pltpu.VMEM((tm, d), BF16)],
        compiler_params=_cparams(("parallel", "arbitrary")),
    )(x, g.reshape(1, d), w)


def _mm_res_kernel(*refs, n_in):
    ys, ws = refs[:n_in], refs[n_in:2 * n_in]
    res_ref, o_ref = refs[2 * n_in], refs[2 * n_in + 1]
    acc = res_ref[...]
    for y_ref, w_ref in zip(ys, ws):
        acc = acc + _dot(y_ref[...], w_ref[...])
    o_ref[...] = acc


def matmul_residual(ys, ws, res, *, tm=1024, tn=512):
    m, n = res.shape
    tm, tn = _tile(m, tm), _tile(n, tn)
    n_in = len(ys)
    in_specs = [pl.BlockSpec((tm, y.shape[1]), lambda i, j: (i, 0)) for y in ys]
    in_specs += [pl.BlockSpec((w.shape[0], tn), lambda i, j: (0, j)) for w in ws]
    in_specs += [pl.BlockSpec((tm, tn), lambda i, j: (i, j))]
    return pl.pallas_call(
        functools.partial(_mm_res_kernel, n_in=n_in),
        grid=(m // tm, n // tn),
        in_specs=in_specs,
        out_specs=pl.BlockSpec((tm, tn), lambda i, j: (i, j)),
        out_shape=jax.ShapeDtypeStruct((m, n), F32),
        compiler_params=_cparams(("parallel", "arbitrary")),
    )(*ys, *ws, res)


def _lru_kernel(x_ref, gate_ref, cw_ref, cb_ref, wa_ref, ba_ref, wx_ref, bx_ref, lam_ref,
                o_ref, tail_ref, h_ref, *, tm):
    @pl.when(pl.program_id(1) == 0)
    def _():
        tail_ref[...] = jnp.zeros_like(tail_ref)
        h_ref[...] = jnp.zeros_like(h_ref)

    x = x_ref[...]
    xx = jnp.concatenate([tail_ref[...], x], axis=0)
    tail_ref[...] = x[tm - 8:, :]
    cw = cw_ref[...]
    xc = cb_ref[...] + cw[CONV_WIDTH - 1:CONV_WIDTH, :] * x
    for s in range(1, CONV_WIDTH):
        xc = xc + cw[CONV_WIDTH - 1 - s:CONV_WIDTH - s, :] * pltpu.roll(xx, s, axis=0)[8:, :]

    xb = xc.astype(BF16)
    gate_a = _dot(xb, wa_ref[...]) + ba_ref[...]
    gate_x = _dot(xb, wx_ref[...]) + bx_ref[...]
    log_a = -LRU_C * _sigmoid(gate_a) * _softplus(-lam_ref[...])
    a = jnp.exp(log_a)
    u = jnp.sqrt(1.0 - a * a) * _sigmoid(gate_x) * xc

    row = lax.broadcasted_iota(jnp.int32, a.shape, 0)
    s = 1
    while s < tm:
        keep = row >= s
        a_s = jnp.where(keep, pltpu.roll(a, s, axis=0), 1.0)
        u_s = jnp.where(keep, pltpu.roll(u, s, axis=0), 0.0)
        u = a * u_s + u
        a = a * a_s
        s *= 2
    h = a * h_ref[...] + u
    h_ref[...] = h[tm - 1:tm, :]
    o_ref[...] = (jax.nn.gelu(gate_ref[...]) * h).astype(o_ref.dtype)


def lru_branch(proj, batch, seq, conv_w, conv_b, wa_bd, ba, wx_bd, bx, lam, *, tm=256):
    m = proj.shape[0]
    w = lam.shape[0]
    tm = _tile(seq, tm)
    nt = seq // tm
    vec = lambda: pl.BlockSpec((1, w), lambda b, i: (0, 0))
    return pl.pallas_call(
        functools.partial(_lru_kernel, tm=tm),
        grid=(batch, nt),
        in_specs=[pl.BlockSpec((tm, w), lambda b, i: (b * nt + i, 0)),
                  pl.BlockSpec((tm, w), lambda b, i: (b * nt + i, 1)),
                  pl.BlockSpec((CONV_WIDTH, w), lambda b, i: (0, 0)),
                  vec(),
                  pl.BlockSpec((w, w), lambda b, i: (0, 0)), vec(),
                  pl.BlockSpec((w, w), lambda b, i: (0, 0)), vec(),
                  vec()],
        out_specs=pl.BlockSpec((tm, w), lambda b, i: (b * nt + i, 0)),
        out_shape=jax.ShapeDtypeStruct((m, w), BF16),
        scratch_shapes=[pltpu.VMEM((8, w), F32), pltpu.VMEM((1, w), F32)],
        compiler_params=_cparams(("parallel", "arbitrary")),
    )(proj, proj, conv_w, conv_b.reshape(1, w), wa_bd, ba.reshape(1, w), wx_bd, bx.reshape(1, w),
      lam.reshape(1, w))


def _rwkv_prep_kernel(pr_ref, pk_ref, pv_ref, pl_ref, qr_ref, qk_ref, qv_ref, ql_ref,
                      mur_ref, muk_ref, muv_ref, mul_ref, w0_ref, w2_ref, a0_ref, a2_ref, g2_ref,
                      kk_ref, ka_ref, bd_ref,
                      r_out, k_out, v_out, a_out, b_out, lw_out, g_out, *, tm, seq):
    first = (pl.program_id(0) * tm) % seq == 0

    def shift_mix(p_ref, q_ref, mu_ref):
        x = p_ref[...]
        prev = jnp.where(first, 0.0, q_ref[7:8, :])
        row = lax.broadcasted_iota(jnp.int32, x.shape, 0)
        xs = jnp.where(row == 0, prev, pltpu.roll(x, 1, axis=0))
        return x + (xs - x) * mu_ref[...]

    r = shift_mix(pr_ref, qr_ref, mur_ref)
    k = shift_mix(pk_ref, qk_ref, muk_ref)
    v = shift_mix(pv_ref, qv_ref, muv_ref)
    lo = shift_mix(pl_ref, ql_ref, mul_ref)

    wlog = -_softplus(-(w0_ref[...] + _dot(jnp.tanh(lo).astype(BF16), w2_ref[...]))) - 0.5
    a = _sigmoid(a0_ref[...] + _dot(lo.astype(BF16), a2_ref[...]))
    g = _dot(_sigmoid(lo).astype(BF16), g2_ref[...])

    kk = k * kk_ref[...]
    nrm = jnp.sqrt(_segsum(kk * kk, bd_ref[...]))
    kk = kk / jnp.maximum(nrm, 1e-12)

    r_out[...] = r
    k_out[...] = k * (1.0 + (a - 1.0) * ka_ref[...])
    v_out[...] = v
    a_out[...] = -kk
    b_out[...] = kk * a
    lw_out[...] = -jnp.exp(wlog)
    g_out[...] = g


def rwkv_prep(proj, seq, col0, mu, w0, w2p, a0, a2p, g2p, k_k, k_a, bd, *, tm=512):
    m = proj.shape[0]
    w = w0.shape[0]
    lw = w2p.shape[0]
    tm = _tile(seq, tm)
    cb = col0 // w
    lb = (col0 + 3 * w) // lw
    main = lambda c, width: pl.BlockSpec((tm, width), lambda i: (i, c))
    prev = lambda c, width: pl.BlockSpec((8, width), lambda i: (jnp.maximum(i * (tm // 8) - 1, 0), c))
    vec = lambda width: pl.BlockSpec((1, width), lambda i: (0, 0))
    mat = lambda a: pl.BlockSpec(a.shape, lambda i: (0, 0))
    mu_r, mu_k, mu_v, mu_l = (mu[None, 0:w], mu[None, w:2 * w], mu[None, 2 * w:3 * w], mu[None, 3 * w:])
    outs = [jax.ShapeDtypeStruct((m, w), F32)] * 7
    return pl.pallas_call(
        functools.partial(_rwkv_prep_kernel, tm=tm, seq=seq),
        grid=(m // tm,),
        in_specs=[main(cb, w), main(cb + 1, w), main(cb + 2, w), main(lb, lw),
                  prev(cb, w), prev(cb + 1, w), prev(cb + 2, w), prev(lb, lw),
                  vec(w), vec(w), vec(w), vec(lw),
                  vec(w), mat(w2p), vec(w), mat(a2p), mat(g2p), vec(w), vec(w), mat(bd)],
        out_specs=[pl.BlockSpec((tm, w), lambda i: (i, 0))] * 7,
        out_shape=outs,
        compiler_params=_cparams(("parallel",)),
    )(proj, proj, proj, proj, proj, proj, proj, proj, mu_r, mu_k, mu_v, mu_l,
      w0.reshape(1, w), w2p, a0.reshape(1, w), a2p, g2p, k_k.reshape(1, w), k_a.reshape(1, w), bd)


def _rwkv_scan_kernel(r_ref, k_ref, v_ref, a_ref, b_ref, lw_ref, g_ref, rk_ref, gnw_ref, gnb_ref,
                      bd_ref, o_ref, s_ref, *, batch, width):
    @pl.when(pl.program_id(0) == 0)
    def _():
        s_ref[...] = jnp.zeros_like(s_ref)

    c = CHUNK
    tri = (lax.broadcasted_iota(jnp.int32, (c, c), 0) >= lax.broadcasted_iota(jnp.int32, (c, c), 1))
    lane = lax.broadcasted_iota(jnp.int32, (c, LANES), 1)
    head0 = lane < HEAD
    i2 = lax.broadcasted_iota(jnp.int32, (2 * c, 2 * c), 0)
    j2 = lax.broadcasted_iota(jnp.int32, (2 * c, 2 * c), 1)
    strict = i2 > j2
    incl = i2 >= j2
    bd = bd_ref[...]
    n_steps = int(math.log2(c))

    def stack(x):
        xb = x.astype(BF16)
        zero = jnp.zeros_like(xb)
        return jnp.concatenate([jnp.where(head0, xb, zero), jnp.where(head0, zero, xb)], axis=0)

    chains = []
    for bi in range(batch):
        lw = lw_ref[bi]
        cum = _dot(tri.astype(F32), lw, precision=HIGHEST)
        tot = cum[c - 1:c, :]
        g_out = jnp.exp(-cum)
        g_suf = jnp.exp(tot - cum)
        g_tot = jnp.exp(tot)
        r, k, v, b = r_ref[bi], k_ref[bi], v_ref[bi], b_ref[bi]
        rt = r * jnp.exp(cum)
        kt = k * g_out
        at = a_ref[bi] * jnp.exp(cum - lw)
        bt = b * g_out
        ks = k * g_suf
        bs = b * g_suf
        rkv = r * k * rk_ref[...]
        for p in range(width // LANES):
            sl = slice(p * LANES, (p + 1) * LANES)
            chains.append(dict(
                bi=bi, p=p, sl=sl, v=v[:, sl], rkv=rkv[:, sl], g_tot=g_tot[:, sl],
                v_s=stack(v[:, sl]),
                ar=jnp.concatenate([stack(at[:, sl]), stack(rt[:, sl])], axis=0),
                kb=jnp.concatenate([stack(kt[:, sl]), stack(bt[:, sl])], axis=0),
                suf=jnp.concatenate([stack(ks[:, sl]), stack(bs[:, sl])], axis=0)))

    for ch in chains:
        ch["state"] = s_ref[ch["bi"], ch["p"]]
        ch["gram"] = _nt(ch["ar"], ch["kb"])
        ch["xs"] = _nt(ch["ar"], ch["state"].astype(BF16))
    for ch in chains:
        gram = ch["gram"]
        ak = jnp.where(strict, gram[:2 * c, :2 * c], 0.0)
        rk = jnp.where(incl, gram[2 * c:, :2 * c], 0.0)
        ch["lpow"] = jnp.where(strict, gram[:2 * c, 2 * c:], 0.0)
        ch["rb"] = jnp.where(incl, gram[2 * c:, 2 * c:], 0.0).astype(BF16)
        ch["lv"] = _dot(jnp.concatenate([ak, rk], axis=0).astype(BF16), ch["v_s"])
    for ch in chains:
        ch["u"] = ch["xs"][:2 * c] + ch["lv"][:2 * c]
    for i in range(n_steps):
        for ch in chains:
            lb = ch["lpow"].astype(BF16)
            ch["u"] = ch["u"] + _dot(lb, ch["u"].astype(BF16))
            if i + 1 < n_steps:
                ch["lpow"] = _dot(lb, lb)
    for ch in chains:
        ub = ch["u"].astype(BF16)
        o_s = ch["xs"][2 * c:] + ch["lv"][2 * c:] + _dot(ch["rb"], ub)
        ch["o"] = o_s[:c] + o_s[c:]
        upd = _tn(jnp.concatenate([ch["v_s"], ub], axis=0), ch["suf"])
        s_ref[ch["bi"], ch["p"]] = ch["state"] * ch["g_tot"] + upd
    for ch in chains:
        bi, sl, o = ch["bi"], ch["sl"], ch["o"]
        mean = _segsum(o, bd) * (1.0 / HEAD)
        d = o - mean
        var = _segsum(d * d, bd) * (1.0 / HEAD)
        on = d * lax.rsqrt(var + GN_EPS) * gnw_ref[:, sl] + gnb_ref[:, sl]
        bonus = _segsum(ch["rkv"], bd) * ch["v"]
        o_ref[bi, :, sl] = ((on + bonus) * g_ref[bi, :, sl]).astype(o_ref.dtype)


def rwkv_scan(r, k, v, a, b, lw, g, batch, seq, r_k, gn_w, gn_b, bd128):
    m, w = r.shape
    nc = seq // CHUNK
    blk = lambda: pl.BlockSpec((batch, CHUNK, w), lambda ci: (0, ci, 0))
    vec = lambda: pl.BlockSpec((1, w), lambda ci: (0, 0))
    as3d = lambda x: x.reshape(batch, seq, w)
    out = pl.pallas_call(
        functools.partial(_rwkv_scan_kernel, batch=batch, width=w),
        grid=(nc,),
        in_specs=[blk() for _ in range(7)] + [vec(), vec(), vec(),
                                               pl.BlockSpec((LANES, LANES), lambda ci: (0, 0))],
        out_specs=blk(),
        out_shape=jax.ShapeDtypeStruct((batch, seq, w), BF16),
        scratch_shapes=[pltpu.VMEM((batch, w // LANES, LANES, LANES), F32)],
        compiler_params=_cparams(("arbitrary",)),
    )(as3d(r), as3d(k), as3d(v), as3d(a), as3d(b), as3d(lw), as3d(g),
      r_k.reshape(1, w), gn_w.reshape(1, w), gn_b.reshape(1, w), bd128)
    return out.reshape(m, w)


def _ffn_kernel(x_ref, g_ref, wg_ref, wu_ref, wd_ref, o_ref, h_ref, acc_ref):
    f = pl.program_id(1)

    @pl.when(f == 0)
    def _():
        h_ref[...] = _rms(x_ref[...], g_ref[...]).astype(BF16)
        acc_ref[...] = x_ref[...]

    h = h_ref[...]
    gate = _dot(h, wg_ref[...])
    up = _dot(h, wu_ref[...])
    act = (gate * _sigmoid(gate) * up).astype(BF16)
    acc_ref[...] += _dot(act, wd_ref[...])

    @pl.when(f == pl.num_programs(1) - 1)
    def _():
        o_ref[...] = acc_ref[...]


def ffn(x, g, wg, wu, wd, *, tm=1024, tf=256):
    m, d = x.shape
    f = wg.shape[1]
    tm, tf = _tile(m, tm), _tile(f, tf)
    return pl.pallas_call(
        _ffn_kernel,
        grid=(m // tm, f // tf),
        in_specs=[pl.BlockSpec((tm, d), lambda i, j: (i, 0)),
                  pl.BlockSpec((1, d), lambda i, j: (0, 0)),
                  pl.BlockSpec((d, tf), lambda i, j: (0, j)),
                  pl.BlockSpec((d, tf), lambda i, j: (0, j)),
                  pl.BlockSpec((tf, d), lambda i, j: (j, 0))],
        out_specs=pl.BlockSpec((tm, d), lambda i, j: (i, 0)),
        out_shape=jax.ShapeDtypeStruct((m, d), F32),
        scratch_shapes=[pltpu.VMEM((tm, d), BF16), pltpu.VMEM((tm, d), F32)],
        compiler_params=_cparams(("parallel", "arbitrary")),
    )(x, g.reshape(1, d), wg, wu, wd)


def _qkv_kernel(x_ref, g_ref, w_ref, pos_ref, gain_ref, freq_ref, bd_ref, o_ref,
                h_ref, cos_ref, sina_ref, sinb_ref, *, n_rot_tiles, tn):
    j = pl.program_id(1)

    @pl.when(j == 0)
    def _():
        h_ref[...] = _rms(x_ref[...], g_ref[...]).astype(BF16)
        ang = pos_ref[...] * freq_ref[...]
        seg = lax.broadcasted_iota(jnp.int32, ang.shape, 1) % HEAD
        half = ROPE_DIM // 2
        cos_ref[...] = jnp.where(seg < ROPE_DIM, jnp.cos(ang), 1.0)
        sin = jnp.sin(ang)
        sina_ref[...] = jnp.where(seg < half, -sin, 0.0)
        sinb_ref[...] = jnp.where((seg >= half) & (seg < ROPE_DIM), sin, 0.0)

    y = _dot(h_ref[...], w_ref[...])

    @pl.when(j < n_rot_tiles)
    def _():
        half = ROPE_DIM // 2
        for c in range(tn // LANES):
            sl = slice(c * LANES, (c + 1) * LANES)
            yc = y[:, sl]
            ms = _segsum(yc * yc, bd_ref[...]) * (1.0 / HEAD)
            yn = yc * lax.rsqrt(ms + RMS_EPS) * gain_ref[:, sl]
            rot = (yn * cos_ref[...] + pltpu.roll(yn, LANES - half, axis=1) * sina_ref[...]
                   + pltpu.roll(yn, half, axis=1) * sinb_ref[...])
            o_ref[:, sl] = rot.astype(o_ref.dtype)

    @pl.when(j >= n_rot_tiles)
    def _():
        o_ref[...] = y.astype(o_ref.dtype)


def qkv_project(x, g, w, pos, gains, freq, bd128, n_rot_cols, *, tm=1024, tn=512):
    m, d = x.shape
    n = w.shape[1]
    tm, tn = _tile(m, tm), _tile(n, tn)
    assert n_rot_cols % tn == 0
    return pl.pallas_call(
        functools.partial(_qkv_kernel, n_rot_tiles=n_rot_cols // tn, tn=tn),
        grid=(m // tm, n // tn),
        in_specs=[pl.BlockSpec((tm, d), lambda i, j: (i, 0)),
                  pl.BlockSpec((1, d), lambda i, j: (0, 0)),
                  pl.BlockSpec((d, tn), lambda i, j: (0, j)),
                  pl.BlockSpec((tm, 1), lambda i, j: (i, 0)),
                  pl.BlockSpec((1, tn), lambda i, j: (0, j)),
                  pl.BlockSpec((1, LANES), lambda i, j: (0, 0)),
                  pl.BlockSpec((LANES, LANES), lambda i, j: (0, 0))],
        out_specs=pl.BlockSpec((tm, tn), lambda i, j: (i, j)),
        out_shape=jax.ShapeDtypeStruct((m, n), BF16),
        scratch_shapes=[pltpu.VMEM((tm, d), BF16), pltpu.VMEM((tm, LANES), F32),
                        pltpu.VMEM((tm, LANES), F32), pltpu.VMEM((tm, LANES), F32)],
        compiler_params=_cparams(("parallel", "arbitrary")),
    )(x, g.reshape(1, d), w, pos, gains, freq, bd128)


def _attn_kernel(fast_ref, q_ref, k_ref, v_ref, lam_ref, subln_ref, o_ref, m_ref, ls_ref, l_ref, acc_ref,
                 *, tq, lambda_init):
    qi = pl.program_id(2)
    q = q_ref[...]
    lane = lax.broadcasted_iota(jnp.int32, q.shape, 1)
    zero = jnp.zeros_like(q)
    qs = jnp.concatenate([jnp.where(lane < HEAD, q, zero), jnp.where(lane < HEAD, zero, q)], axis=0)
    acc_ref[...] = jnp.zeros_like(acc_ref)

    def scores(j, masked):
        start = pl.multiple_of(j * tq, tq)
        s = _nt(qs, k_ref[pl.ds(start, tq), :])
        if masked:
            row = lax.broadcasted_iota(jnp.int32, (tq, tq), 0)
            col = lax.broadcasted_iota(jnp.int32, (tq, tq), 1)
            keep = jnp.concatenate([col <= row, col <= row], axis=0)
            s = jnp.where(keep, s, NEG_BIG)
        return s, v_ref[pl.ds(start, tq), :]

    def sweep(step):
        def body(j, carry):
            step(j, False)
            return carry
        lax.fori_loop(0, qi, body, 0)
        step(qi, True)

    @pl.when(fast_ref[0] == 1)
    def _():
        l_ref[...] = jnp.zeros_like(l_ref)

        def step(j, masked):
            s, vb = scores(j, masked)
            p = jnp.exp2(s)
            part = p[:, 0:LANES]
            for c in range(1, tq // LANES):
                part = part + p[:, c * LANES:(c + 1) * LANES]
            l_ref[...] += part
            acc_ref[...] += _dot(p.astype(BF16), vb)

        sweep(step)
        ls_ref[...] = jnp.sum(l_ref[...], axis=-1, keepdims=True)

    @pl.when(fast_ref[0] == 0)
    def _():
        m_ref[...] = jnp.full_like(m_ref, NEG_BIG)
        ls_ref[...] = jnp.zeros_like(ls_ref)

        def step(j, masked):
            s, vb = scores(j, masked)
            m_old = m_ref[...]
            m_new = jnp.maximum(m_old, jnp.max(s, axis=-1, keepdims=True))
            alpha = jnp.exp2(m_old - m_new)
            p = jnp.exp2(s - m_new)
            ls_ref[...] = alpha * ls_ref[...] + jnp.sum(p, axis=-1, keepdims=True)
            acc_ref[...] = alpha * acc_ref[...] + _dot(p.astype(BF16), vb)
            m_ref[...] = m_new

        sweep(step)

    lq = lam_ref[...]
    lam = (jnp.exp(jnp.sum(lq[0:1] * lq[1:2], axis=-1, keepdims=True))
           - jnp.exp(jnp.sum(lq[2:3] * lq[3:4], axis=-1, keepdims=True)) + lambda_init)
    o = acc_ref[...] / ls_ref[...]
    o = o[:tq] - lam * o[tq:]
    o = _rms(o, subln_ref[...]) * (1.0 - lambda_init)
    o_ref[...] = o.astype(o_ref.dtype)


def diff_attention(qkv, fast, batch, seq, n_heads, lam_params, subln, lambda_init, *, tq=512):
    m = qkv.shape[0]
    tq = _tile(seq, tq)
    nq = seq // tq
    grid_spec = pltpu.PrefetchScalarGridSpec(
        num_scalar_prefetch=1,
        grid=(batch, n_heads, nq),
        in_specs=[pl.BlockSpec((tq, LANES), lambda b, h, i, f: (b * nq + i, h)),
                  pl.BlockSpec((seq, LANES), lambda b, h, i, f: (b, n_heads + h)),
                  pl.BlockSpec((seq, LANES), lambda b, h, i, f: (b, 2 * n_heads + h)),
                  pl.BlockSpec((4, HEAD), lambda b, h, i, f: (0, 0)),
                  pl.BlockSpec((1, LANES), lambda b, h, i, f: (0, 0))],
        out_specs=pl.BlockSpec((tq, LANES), lambda b, h, i, f: (b * nq + i, h)),
        scratch_shapes=[pltpu.VMEM((2 * tq, 1), F32), pltpu.VMEM((2 * tq, 1), F32),
                        pltpu.VMEM((2 * tq, LANES), F32), pltpu.VMEM((2 * tq, LANES), F32)])
    return pl.pallas_call(
        functools.partial(_attn_kernel, tq=tq, lambda_init=lambda_init),
        grid_spec=grid_spec,
        out_shape=jax.ShapeDtypeStruct((m, n_heads * LANES), BF16),
        compiler_params=_cparams(("parallel", "parallel", "arbitrary")),
    )(fast, qkv, qkv, qkv, lam_params, subln.reshape(1, LANES))


def _router_kernel(x_ref, g_ref, router_ref, ids_ref, gates_ref):
    h = _rms(x_ref[...], g_ref[...])
    logits = _dot(h, router_ref[...], precision=HIGHEST)
    lane = lax.broadcasted_iota(jnp.int32, logits.shape, 1)
    logits = jnp.where(lane < N_EXPERTS, logits, NEG_BIG)
    v1 = jnp.max(logits, axis=-1, keepdims=True)
    i1 = jnp.min(jnp.where(logits == v1, lane, LANES), axis=-1, keepdims=True)
    rest = jnp.where(lane == i1, NEG_BIG, logits)
    v2 = jnp.max(rest, axis=-1, keepdims=True)
    i2 = jnp.min(jnp.where(rest == v2, lane, LANES), axis=-1, keepdims=True)
    e2 = jnp.exp(v2 - v1)
    ids_ref[...] = jnp.where(lane == 0, i1, i2)
    gates_ref[...] = jnp.where(lane == 0, 1.0 / (1.0 + e2), e2 / (1.0 + e2))


def moe_router(x, g, router_p, *, tm=1024):
    m, d = x.shape
    tm = _tile(m, tm)
    return pl.pallas_call(
        _router_kernel,
        grid=(m // tm,),
        in_specs=[pl.BlockSpec((tm, d), lambda i: (i, 0)),
                  pl.BlockSpec((1, d), lambda i: (0, 0)),
                  pl.BlockSpec((d, LANES), lambda i: (0, 0))],
        out_specs=[pl.BlockSpec((tm, LANES), lambda i: (i, 0))] * 2,
        out_shape=[jax.ShapeDtypeStruct((m, LANES), jnp.int32), jax.ShapeDtypeStruct((m, LANES), F32)],
        compiler_params=_cparams(("parallel",)),
    )(x, g.reshape(1, d), router_p)


def _route_tables(ids, tm, n_tiles):
    n_pairs = ids.shape[0] * 2
    n_rows = n_tiles * tm
    shift = max(n_pairs, n_rows).bit_length()
    e_flat = ids.reshape(-1)
    experts = jnp.arange(N_EXPERTS, dtype=jnp.int32)
    counts = jnp.sum((e_flat[:, None] == experts[None, :]).astype(jnp.int32), axis=0)
    padded = ((counts + tm - 1) // tm) * tm
    ends = jnp.cumsum(padded)
    pad_ends = jnp.cumsum(padded - counts)
    q = jnp.arange(n_rows - n_pairs, dtype=jnp.int32)
    pad_expert = jnp.sum((q[:, None] >= pad_ends[None, :]).astype(jnp.int32), axis=1)
    low = (1 << shift) - 1
    keys = jnp.concatenate([(e_flat << shift) | jnp.arange(n_pairs, dtype=jnp.int32),
                            (pad_expert << shift) | low])
    keys = jnp.sort(keys)
    perm = jnp.where((keys & low) == low, -1, keys & low)
    starts = jnp.arange(n_tiles, dtype=jnp.int32) * tm
    tile_expert = jnp.sum((starts[:, None] >= ends[None, :]).astype(jnp.int32), axis=1)
    n_valid = ends[-1] // tm
    last_expert = jnp.sum(jnp.where(jnp.arange(n_tiles) == n_valid - 1, tile_expert, 0))
    tile_expert = jnp.where(jnp.arange(n_tiles) < n_valid, tile_expert, last_expert)
    n_real = jnp.sum((perm >= 0).reshape(n_tiles, tm).astype(jnp.int32), axis=1)
    return perm, tile_expert.astype(jnp.int32), n_valid.reshape(1).astype(jnp.int32), n_real


def _moe_group_kernel(te_ref, nv_ref, perm_ref, nr_ref, x_hbm, g_ref, wg_ref, wu_ref, wd_ref, y_hbm,
                      xbuf, ybuf, h_ref, acc_ref, gsem, ssem, *, tm, n_tokens):
    i = pl.program_id(0)
    f = pl.program_id(1)
    valid = i < nv_ref[0]
    base = i * tm
    d = h_ref.shape[1]
    n_chunks = d // LANES

    def tile_rows(k):
        return pl.ds(pl.multiple_of(k * SUBLANES, SUBLANES), SUBLANES)

    slot = i % 2

    def tile_in(s, r, t):
        return pltpu.make_async_copy(x_hbm.at[tile_rows(t), :], xbuf.at[s, tile_rows(r), :], gsem.at[s])

    def tile_out(r, dst):
        return pltpu.make_async_copy(ybuf.at[tile_rows(r), :], y_hbm.at[tile_rows(dst), :], ssem)

    def chunk_rows(c):
        return pl.ds(c, tm, stride=SUBLANES)

    def start_gather(tile, s):
        def start(r, carry):
            j = perm_ref[tile * tm + r]
            tile_in(s, r, jnp.maximum(j, 0) >> 1).start()
            return carry
        lax.fori_loop(0, tm, start, 0, unroll=8)

    def wait_scatter(count):
        def wait(r, carry):
            tile_out(r, 0).wait()
            return carry
        lax.fori_loop(0, count, wait, 0)

    @pl.when(valid & (f == 0))
    def _():
        @pl.when(i == 0)
        def _():
            start_gather(0, 0)

        def wait(r, carry):
            tile_in(slot, r, 0).wait()
            return carry
        lax.fori_loop(0, tm, wait, 0, unroll=8)

        @pl.when(i + 1 < nv_ref[0])
        def _():
            start_gather(i + 1, 1 - slot)

        ss = jnp.zeros((tm, 1), F32)
        for c in range(n_chunks):
            xc = xbuf[slot, chunk_rows(c), :]
            ss = ss + jnp.sum(xc * xc, axis=-1, keepdims=True)
        inv = lax.rsqrt(ss * (1.0 / d) + RMS_EPS)
        for c in range(n_chunks):
            sl = slice(c * LANES, (c + 1) * LANES)
            h_ref[:, sl] = (xbuf[slot, chunk_rows(c), :] * inv * g_ref[:, sl]).astype(BF16)
        acc_ref[...] = jnp.zeros_like(acc_ref)

    @pl.when(valid)
    def _():
        h = h_ref[...]
        gate = _dot(h, wg_ref[...])
        up = _dot(h, wu_ref[...])
        act = (gate * _sigmoid(gate) * up).astype(BF16)
        acc_ref[...] += _dot(act, wd_ref[...])

    @pl.when(valid & (f == pl.num_programs(1) - 1))
    def _():
        @pl.when(i > 0)
        def _():
            wait_scatter(nr_ref[jnp.maximum(i - 1, 0)])

        for c in range(n_chunks):
            ybuf[chunk_rows(c), :] = acc_ref[:, c * LANES:(c + 1) * LANES]

        n_real = nr_ref[i]

        def start(r):
            j = perm_ref[base + r]
            tile_out(r, (j & 1) * n_tokens + (j >> 1)).start()

        def start8(r8, carry):
            for k in range(SUBLANES):
                start(r8 * SUBLANES + k)
            return carry
        lax.fori_loop(0, n_real // SUBLANES, start8, 0)

        def start1(r, carry):
            start(r)
            return carry
        lax.fori_loop((n_real // SUBLANES) * SUBLANES, n_real, start1, 0)

        @pl.when(i == nv_ref[0] - 1)
        def _():
            wait_scatter(n_real)


def moe_experts(x8, g, perm, tile_expert, n_valid, n_real, wg, wu, wd, *, tm, tf=512):
    d = g.shape[0]
    m = x8.shape[0] // SUBLANES
    f = wg.shape[2]
    tf = _tile(f, tf)
    n_tiles, n_f = tile_expert.shape[0], f // tf
    fidx = lambda i, j, nv: jnp.where(i < nv[0], j, n_f - 1)
    grid_spec = pltpu.PrefetchScalarGridSpec(
        num_scalar_prefetch=4,
        grid=(n_tiles, n_f),
        in_specs=[pl.BlockSpec(memory_space=pl.ANY),
                  pl.BlockSpec((1, d), lambda i, j, te, nv, pm, nr: (0, 0)),
                  pl.BlockSpec((None, d, tf), lambda i, j, te, nv, pm, nr: (te[i], 0, fidx(i, j, nv))),
                  pl.BlockSpec((None, d, tf), lambda i, j, te, nv, pm, nr: (te[i], 0, fidx(i, j, nv))),
                  pl.BlockSpec((None, tf, d), lambda i, j, te, nv, pm, nr: (te[i], fidx(i, j, nv), 0))],
        out_specs=pl.BlockSpec(memory_space=pl.ANY),
        scratch_shapes=[pltpu.VMEM((2, tm * SUBLANES, LANES), F32), pltpu.VMEM((tm * SUBLANES, LANES), F32),
                        pltpu.VMEM((tm, d), BF16), pltpu.VMEM((tm, d), F32),
                        pltpu.SemaphoreType.DMA((2,)), pltpu.SemaphoreType.DMA(())])
    return pl.pallas_call(
        functools.partial(_moe_group_kernel, tm=tm, n_tokens=m),
        grid_spec=grid_spec,
        out_shape=jax.ShapeDtypeStruct((2 * m * SUBLANES, LANES), F32),
        compiler_params=_cparams(("arbitrary", "arbitrary")),
    )(tile_expert, n_valid, perm, n_real, x8, g.reshape(1, d), wg, wu, wd)


def _moe_combine_kernel(x_ref, y0_ref, y1_ref, gates_ref, o_ref, *, tm):
    gates = gates_ref[...]
    g0, g1 = gates[:, 0:1], gates[:, 1:2]
    for c in range(x_ref.shape[1] // LANES):
        sl = slice(c * LANES, (c + 1) * LANES)
        rows = pl.ds(c, tm, stride=SUBLANES)
        o_ref[:, sl] = x_ref[:, sl] + g0 * y0_ref[rows, :] + g1 * y1_ref[rows, :]


def moe_combine(x, y8, gates, *, tm=1024):
    m, d = x.shape
    tm = _tile(m, tm)
    nb = m // tm
    return pl.pallas_call(
        functools.partial(_moe_combine_kernel, tm=tm),
        grid=(nb,),
        in_specs=[pl.BlockSpec((tm, d), lambda i: (i, 0)),
                  pl.BlockSpec((tm * SUBLANES, LANES), lambda i: (i, 0)),
                  pl.BlockSpec((tm * SUBLANES, LANES), lambda i: (nb + i, 0)),
                  pl.BlockSpec((tm, LANES), lambda i: (i, 0))],
        out_specs=pl.BlockSpec((tm, d), lambda i: (i, 0)),
        out_shape=jax.ShapeDtypeStruct((m, d), F32),
        compiler_params=_cparams(("parallel",)),
    )(x, y8, y8, gates)


def moe(x, g, router_p, wg, wu, wd, *, tm=1024):
    m, d = x.shape
    assert d == SUBLANES * LANES
    tm = _tile(m, tm)
    n_tiles = (2 * m) // tm + N_EXPERTS
    ids, gates = moe_router(x, g, router_p)
    perm, tile_expert, n_valid, n_real = _route_tables(ids[:, :2], tm, n_tiles)
    x8 = x.reshape(m * SUBLANES, LANES)
    y8 = moe_experts(x8, g, perm, tile_expert, n_valid, n_real, wg, wu, wd, tm=tm)
    return moe_combine(x, y8, gates, tm=tm)


def _block_diag(blocks):
    n, h, _ = blocks.shape
    eye = jnp.eye(n, dtype=blocks.dtype)
    return (eye[:, None, :, None] * blocks[:, :, None, :]).reshape(n * h, n * h)


def _seg_ones(n):
    seg = jnp.arange(n) // HEAD
    return (seg[:, None] == seg[None, :]).astype(BF16)


def _even_layer(x, batch, seq, ln_mix, w_in, conv_w, conv_b, gate_a_w, gate_a_b, gate_x_w, gate_x_b,
                lru_lambda, shift_mu, w0, w2, a0, a2, g2, k_k, k_a, r_k, gn_w, gn_b, w_out,
                ln_ffn, ffn_gate, ffn_up, ffn_down):
    lru_w = lru_lambda.shape[0]
    rw_w = w0.shape[0]
    dl, al, gl = w2.shape[0], a2.shape[0], g2.shape[0]
    n_in = w_in.shape[1]
    proj = norm_matmul(x, ln_mix, w_in.astype(BF16), tn=n_in // 2)

    y_lru = lru_branch(proj, batch, seq, conv_w, conv_b,
                       _block_diag(gate_a_w).astype(BF16), gate_a_b,
                       _block_diag(gate_x_w).astype(BF16), gate_x_b, lru_lambda)

    lora = dl + al + gl
    zeros = lambda n: jnp.zeros((n, rw_w), F32)
    w2p = jnp.concatenate([w2, zeros(al + gl)], axis=0).astype(BF16)
    a2p = jnp.concatenate([zeros(dl), a2, zeros(gl)], axis=0).astype(BF16)
    g2p = jnp.concatenate([zeros(dl + al), g2], axis=0).astype(BF16)
    assert lora == w2p.shape[0]
    r, k, v, a, b, lw, g = rwkv_prep(proj, seq, 2 * lru_w, shift_mu, w0, w2p, a0, a2p, g2p, k_k, k_a,
                                     _seg_ones(rw_w))
    y_rwkv = rwkv_scan(r, k, v, a, b, lw, g, batch, seq, r_k.reshape(-1), gn_w, gn_b, _seg_ones(LANES))

    w_out = w_out.astype(BF16)
    x = matmul_residual([y_lru, y_rwkv], [w_out[:lru_w], w_out[lru_w:]], x)
    return ffn(x, ln_ffn, ffn_gate.astype(BF16), ffn_up.astype(BF16), ffn_down.astype(BF16))


def _odd_layer(x, pos, batch, seq, layer_idx, ln_mix, w_qkv, q_norm, k_norm, lq1, lk1, lq2, lk2, subln,
               w_o, ln_ffn, router, moe_gate, moe_up, moe_down):
    d = x.shape[1]
    n_heads = d // (2 * HEAD)
    qd = n_heads * 2 * HEAD
    lambda_init = 0.8 - 0.6 * math.exp(-0.3 * layer_idx)
    reps = qd // HEAD
    q_gain = q_norm * (HEAD ** -0.5 * LOG2E)
    logit_bound = 1.02 * HEAD * jnp.max(jnp.abs(q_gain)) * jnp.max(jnp.abs(k_norm))
    fast = (logit_bound <= MAX_EXP2_ARG).astype(jnp.int32).reshape(1)
    gains = jnp.concatenate([jnp.tile(q_gain, reps),
                             jnp.tile(k_norm, reps),
                             jnp.ones((w_qkv.shape[1] - 2 * qd,), F32)])[None, :]
    seg = jnp.arange(LANES) % HEAD
    inv_freq = ROPE_THETA ** (-(2.0 * (seg % (ROPE_DIM // 2))).astype(F32) / ROPE_DIM)
    freq = jnp.where(seg < ROPE_DIM, inv_freq, 0.0)[None, :].astype(F32)
    qkv = qkv_project(x, ln_mix, w_qkv.astype(BF16), pos, gains, freq, _seg_ones(LANES), 2 * qd)
    lam_params = jnp.stack([lq1, lk1, lq2, lk2]).astype(F32)
    o = diff_attention(qkv, fast, batch, seq, n_heads, lam_params, subln, lambda_init)
    x = matmul_residual([o], [w_o.astype(BF16)], x)
    router_p = jnp.pad(router, ((0, 0), (0, LANES - router.shape[1])))
    return moe(x, ln_ffn, router_p, moe_gate.astype(BF16), moe_up.astype(BF16), moe_down.astype(BF16))


def kernel(x, positions, e_ln_mix, e_w_in, e_conv_w, e_conv_b, e_gate_a_w, e_gate_a_b, e_gate_x_w, e_gate_x_b, e_lru_lambda, e_shift_mu, e_w0, e_w2, e_a0, e_a2, e_g2, e_k_k, e_k_a, e_r_k, e_gn_w, e_gn_b, e_w_out, e_ln_ffn, e_ffn_gate, e_ffn_up, e_ffn_down, o_ln_mix, o_w_qkv, o_q_norm, o_k_norm, o_lambda_q1, o_lambda_k1, o_lambda_q2, o_lambda_k2, o_subln, o_w_o, o_ln_ffn, o_router, o_moe_gate, o_moe_up, o_moe_down):
    batch, seq, d = x.shape
    depth = e_ln_mix.shape[0] + o_ln_mix.shape[0]
    xf = x.reshape(batch * seq, d)
    pos = positions.reshape(batch * seq, 1).astype(F32)
    for i in range(depth):
        j = i // 2
        if i % 2 == 0:
            xf = _even_layer(xf, batch, seq, e_ln_mix[j], e_w_in[j], e_conv_w[j], e_conv_b[j],
                             e_gate_a_w[j], e_gate_a_b[j], e_gate_x_w[j], e_gate_x_b[j],
                             e_lru_lambda[j], e_shift_mu[j], e_w0[j], e_w2[j], e_a0[j], e_a2[j],
                             e_g2[j], e_k_k[j], e_k_a[j], e_r_k[j], e_gn_w[j], e_gn_b[j], e_w_out[j],
                             e_ln_ffn[j], e_ffn_gate[j], e_ffn_up[j], e_ffn_down[j])
        else:
            xf = _odd_layer(xf, pos, batch, seq, i, o_ln_mix[j], o_w_qkv[j], o_q_norm[j], o_k_norm[j],
                            o_lambda_q1[j], o_lambda_k1[j], o_lambda_q2[j], o_lambda_k2[j], o_subln[j],
                            o_w_o[j], o_ln_ffn[j], o_router[j], o_moe_gate[j], o_moe_up[j],
                            o_moe_down[j])
    return xf.reshape(batch, seq, d)
```

```python
import functools
import math

import jax
import jax.numpy as jnp
from jax import lax
from jax.experimental import pallas as pl
from jax.experimental.pallas import tpu as pltpu

F32 = jnp.float32
BF16 = jnp.bfloat16
HIGHEST = lax.Precision.HIGHEST

LANES = 128
SUBLANES = 8
VMEM_LIMIT = 56 * 1024 * 1024

HEAD = 64
CONV_WIDTH = 4
LRU_C = 8.0
GN_EPS = 64e-5
RMS_EPS = 1e-6
ROPE_DIM = 16
ROPE_THETA = 500000.0
N_EXPERTS = 8
CHUNK = 64
NEG_BIG = -1e30
LOG2E = 1.4426950408889634
MAX_EXP2_ARG = 60.0


def _cparams(sem):
    return pltpu.CompilerParams(dimension_semantics=sem, vmem_limit_bytes=VMEM_LIMIT)


def _tile(n, pref):
    t = min(n, pref)
    assert n % t == 0, (n, pref)
    return t


def _nt(a, b, **kw):
    return lax.dot_general(a, b, (((1,), (1,)), ((), ())), preferred_element_type=F32, **kw)


def _tn(a, b, **kw):
    return lax.dot_general(a, b, (((0,), (0,)), ((), ())), preferred_element_type=F32, **kw)


def _dot(a, b, **kw):
    return jnp.dot(a, b, preferred_element_type=F32, **kw)


def _segsum(x, bd):
    hi = x.astype(BF16)
    lo = (x - hi.astype(F32)).astype(BF16)
    return _dot(hi, bd) + _dot(lo, bd)


def _rms(x, g):
    ms = jnp.mean(x * x, axis=-1, keepdims=True)
    return x * lax.rsqrt(ms + RMS_EPS) * g


def _sigmoid(x):
    return 1.0 / (1.0 + jnp.exp(-x))


def _softplus(x):
    return jnp.maximum(x, 0.0) + jnp.log1p(jnp.exp(-jnp.abs(x)))


def _norm_mm_kernel(x_ref, g_ref, w_ref, o_ref, h_ref):
    @pl.when(pl.program_id(1) == 0)
    def _():
        h_ref[...] = _rms(x_ref[...], g_ref[...]).astype(BF16)

    o_ref[...] = _dot(h_ref[...], w_ref[...]).astype(o_ref.dtype)


def norm_matmul(x, g, w, *, tm=1024, tn=512, out_dtype=F32):
    m, d = x.shape
    n = w.shape[1]
    tm, tn = _tile(m, tm), _tile(n, tn)
    return pl.pallas_call(
        _norm_mm_kernel,
        grid=(m // tm, n // tn),
        in_specs=[pl.BlockSpec((tm, d), lambda i, j: (i, 0)),
                  pl.BlockSpec((1, d), lambda i, j: (0, 0)),
                  pl.BlockSpec((d, tn), lambda i, j: (0, j))],
        out_specs=pl.BlockSpec((tm, tn), lambda i, j: (i, j)),
        out_shape=jax.ShapeDtypeStruct((m, n), out_dtype),
        scratch_shapes=[pltpu.VMEM((tm, d), BF16)],
        compiler_params=_cparams(("parallel", "arbitrary")),
    )(x, g.reshape(1, d), w)


def _mm_res_kernel(*refs, n_in):
    ys, ws = refs[:n_in], refs[n_in:2 * n_in]
    res_ref, o_ref = refs[2 * n_in], refs[2 * n_in + 1]
    acc = res_ref[...]
    for y_ref, w_ref in zip(ys, ws):
        acc = acc + _dot(y_ref[...], w_ref[...])
    o_ref[...] = acc


def matmul_residual(ys, ws, res, *, tm=1024, tn=512):
    m, n = res.shape
    tm, tn = _tile(m, tm), _tile(n, tn)
    n_in = len(ys)
    in_specs = [pl.BlockSpec((tm, y.shape[1]), lambda i, j: (i, 0)) for y in ys]
    in_specs += [pl.BlockSpec((w.shape[0], tn), lambda i, j: (0, j)) for w in ws]
    in_specs += [pl.BlockSpec((tm, tn), lambda i, j: (i, j))]
    return pl.pallas_call(
        functools.partial(_mm_res_kernel, n_in=n_in),
        grid=(m // tm, n // tn),
        in_specs=in_specs,
        out_specs=pl.BlockSpec((tm, tn), lambda i, j: (i, j)),
        out_shape=jax.ShapeDtypeStruct((m, n), F32),
        compiler_params=_cparams(("parallel", "arbitrary")),
    )(*ys, *ws, res)


def _lru_kernel(x_ref, gate_ref, cw_ref, cb_ref, wa_ref, ba_ref, wx_ref, bx_ref, lam_ref,
                o_ref, tail_ref, h_ref, *, tm):
    @pl.when(pl.program_id(1) == 0)
    def _():
        tail_ref[...] = jnp.zeros_like(tail_ref)
        h_ref[...] = jnp.zeros_like(h_ref)

    x = x_ref[...]
    xx = jnp.concatenate([tail_ref[...], x], axis=0)
    tail_ref[...] = x[tm - 8:, :]
    cw = cw_ref[...]
    xc = cb_ref[...] + cw[CONV_WIDTH - 1:CONV_WIDTH, :] * x
    for s in range(1, CONV_WIDTH):
        xc = xc + cw[CONV_WIDTH - 1 - s:CONV_WIDTH - s, :] * pltpu.roll(xx, s, axis=0)[8:, :]

    xb = xc.astype(BF16)
    gate_a = _dot(xb, wa_ref[...]) + ba_ref[...]
    gate_x = _dot(xb, wx_ref[...]) + bx_ref[...]
    log_a = -LRU_C * _sigmoid(gate_a) * _softplus(-lam_ref[...])
    a = jnp.exp(log_a)
    u = jnp.sqrt(1.0 - a * a) * _sigmoid(gate_x) * xc

    row = lax.broadcasted_iota(jnp.int32, a.shape, 0)
    s = 1
    while s < tm:
        keep = row >= s
        a_s = jnp.where(keep, pltpu.roll(a, s, axis=0), 1.0)
        u_s = jnp.where(keep, pltpu.roll(u, s, axis=0), 0.0)
        u = a * u_s + u
        a = a * a_s
        s *= 2
    h = a * h_ref[...] + u
    h_ref[...] = h[tm - 1:tm, :]
    o_ref[...] = (jax.nn.gelu(gate_ref[...]) * h).astype(o_ref.dtype)


def lru_branch(proj, batch, seq, conv_w, conv_b, wa_bd, ba, wx_bd, bx, lam, *, tm=256):
    m = proj.shape[0]
    w = lam.shape[0]
    tm = _tile(seq, tm)
    nt = seq // tm
    vec = lambda: pl.BlockSpec((1, w), lambda b, i: (0, 0))
    return pl.pallas_call(
        functools.partial(_lru_kernel, tm=tm),
        grid=(batch, nt),
        in_specs=[pl.BlockSpec((tm, w), lambda b, i: (b * nt + i, 0)),
                  pl.BlockSpec((tm, w), lambda b, i: (b * nt + i, 1)),
                  pl.BlockSpec((CONV_WIDTH, w), lambda b, i: (0, 0)),
                  vec(),
                  pl.BlockSpec((w, w), lambda b, i: (0, 0)), vec(),
                  pl.BlockSpec((w, w), lambda b, i: (0, 0)), vec(),
                  vec()],
        out_specs=pl.BlockSpec((tm, w), lambda b, i: (b * nt + i, 0)),
        out_shape=jax.ShapeDtypeStruct((m, w), BF16),
        scratch_shapes=[pltpu.VMEM((8, w), F32), pltpu.VMEM((1, w), F32)],
        compiler_params=_cparams(("parallel", "arbitrary")),
    )(proj, proj, conv_w, conv_b.reshape(1, w), wa_bd, ba.reshape(1, w), wx_bd, bx.reshape(1, w),
      lam.reshape(1, w))


def _rwkv_prep_kernel(pr_ref, pk_ref, pv_ref, pl_ref, qr_ref, qk_ref, qv_ref, ql_ref,
                      mur_ref, muk_ref, muv_ref, mul_ref, w0_ref, w2_ref, a0_ref, a2_ref, g2_ref,
                      kk_ref, ka_ref, bd_ref,
                      r_out, k_out, v_out, a_out, b_out, lw_out, g_out, *, tm, seq):
    first = (pl.program_id(0) * tm) % seq == 0

    def shift_mix(p_ref, q_ref, mu_ref):
        x = p_ref[...]
        prev = jnp.where(first, 0.0, q_ref[7:8, :])
        row = lax.broadcasted_iota(jnp.int32, x.shape, 0)
        xs = jnp.where(row == 0, prev, pltpu.roll(x, 1, axis=0))
        return x + (xs - x) * mu_ref[...]

    r = shift_mix(pr_ref, qr_ref, mur_ref)
    k = shift_mix(pk_ref, qk_ref, muk_ref)
    v = shift_mix(pv_ref, qv_ref, muv_ref)
    lo = shift_mix(pl_ref, ql_ref, mul_ref)

    wlog = -_softplus(-(w0_ref[...] + _dot(jnp.tanh(lo).astype(BF16), w2_ref[...]))) - 0.5
    a = _sigmoid(a0_ref[...] + _dot(lo.astype(BF16), a2_ref[...]))
    g = _dot(_sigmoid(lo).astype(BF16), g2_ref[...])

    kk = k * kk_ref[...]
    nrm = jnp.sqrt(_segsum(kk * kk, bd_ref[...]))
    kk = kk / jnp.maximum(nrm, 1e-12)

    r_out[...] = r
    k_out[...] = k * (1.0 + (a - 1.0) * ka_ref[...])
    v_out[...] = v
    a_out[...] = -kk
    b_out[...] = kk * a
    lw_out[...] = -jnp.exp(wlog)
    g_out[...] = g


def rwkv_prep(proj, seq, col0, mu, w0, w2p, a0, a2p, g2p, k_k, k_a, bd, *, tm=512):
    m = proj.shape[0]
    w = w0.shape[0]
    lw = w2p.shape[0]
    tm = _tile(seq, tm)
    cb = col0 // w
    lb = (col0 + 3 * w) // lw
    main = lambda c, width: pl.BlockSpec((tm, width), lambda i: (i, c))
    prev = lambda c, width: pl.BlockSpec((8, width), lambda i: (jnp.maximum(i * (tm // 8) - 1, 0), c))
    vec = lambda width: pl.BlockSpec((1, width), lambda i: (0, 0))
    mat = lambda a: pl.BlockSpec(a.shape, lambda i: (0, 0))
    mu_r, mu_k, mu_v, mu_l = (mu[None, 0:w], mu[None, w:2 * w], mu[None, 2 * w:3 * w], mu[None, 3 * w:])
    outs = [jax.ShapeDtypeStruct((m, w), F32)] * 7
    return pl.pallas_call(
        functools.partial(_rwkv_prep_kernel, tm=tm, seq=seq),
        grid=(m // tm,),
        in_specs=[main(cb, w), main(cb + 1, w), main(cb + 2, w), main(lb, lw),
                  prev(cb, w), prev(cb + 1, w), prev(cb + 2, w), prev(lb, lw),
                  vec(w), vec(w), vec(w), vec(lw),
                  vec(w), mat(w2p), vec(w), mat(a2p), mat(g2p), vec(w), vec(w), mat(bd)],
        out_specs=[pl.BlockSpec((tm, w), lambda i: (i, 0))] * 7,
        out_shape=outs,
        compiler_params=_cparams(("parallel",)),
    )(proj, proj, proj, proj, proj, proj, proj, proj, mu_r, mu_k, mu_v, mu_l,
      w0.reshape(1, w), w2p, a0.reshape(1, w), a2p, g2p, k_k.reshape(1, w), k_a.reshape(1, w), bd)


def _rwkv_scan_kernel(r_ref, k_ref, v_ref, a_ref, b_ref, lw_ref, g_ref, rk_ref, gnw_ref, gnb_ref,
                      bd_ref, o_ref, s_ref, *, batch, width):
    @pl.when(pl.program_id(0) == 0)
    def _():
        s_ref[...] = jnp.zeros_like(s_ref)

    c = CHUNK
    tri = (lax.broadcasted_iota(jnp.int32, (c, c), 0) >= lax.broadcasted_iota(jnp.int32, (c, c), 1))
    lane = lax.broadcasted_iota(jnp.int32, (c, LANES), 1)
    head0 = lane < HEAD
    i2 = lax.broadcasted_iota(jnp.int32, (2 * c, 2 * c), 0)
    j2 = lax.broadcasted_iota(jnp.int32, (2 * c, 2 * c), 1)
    strict = i2 > j2
    incl = i2 >= j2
    bd = bd_ref[...]
    n_steps = int(math.log2(c))

    def stack(x):
        xb = x.astype(BF16)
        zero = jnp.zeros_like(xb)
        return jnp.concatenate([jnp.where(head0, xb, zero), jnp.where(head0, zero, xb)], axis=0)

    chains = []
    for bi in range(batch):
        lw = lw_ref[bi]
        cum = _dot(tri.astype(F32), lw, precision=HIGHEST)
        tot = cum[c - 1:c, :]
        g_out = jnp.exp(-cum)
        g_suf = jnp.exp(tot - cum)
        g_tot = jnp.exp(tot)
        r, k, v, b = r_ref[bi], k_ref[bi], v_ref[bi], b_ref[bi]
        rt = r * jnp.exp(cum)
        kt = k * g_out
        at = a_ref[bi] * jnp.exp(cum - lw)
        bt = b * g_out
        ks = k * g_suf
        bs = b * g_suf
        rkv = r * k * rk_ref[...]
        for p in range(width // LANES):
            sl = slice(p * LANES, (p + 1) * LANES)
            chains.append(dict(
                bi=bi, p=p, sl=sl, v=v[:, sl], rkv=rkv[:, sl], g_tot=g_tot[:, sl],
                v_s=stack(v[:, sl]),
                ar=jnp.concatenate([stack(at[:, sl]), stack(rt[:, sl])], axis=0),
                kb=jnp.concatenate([stack(kt[:, sl]), stack(bt[:, sl])], axis=0),
                suf=jnp.concatenate([stack(ks[:, sl]), stack(bs[:, sl])], axis=0)))

    for ch in chains:
        ch["state"] = s_ref[ch["bi"], ch["p"]]
        ch["gram"] = _nt(ch["ar"], ch["kb"])
        ch["xs"] = _nt(ch["ar"], ch["state"].astype(BF16))
    for ch in chains:
        gram = ch["gram"]
        ak = jnp.where(strict, gram[:2 * c, :2 * c], 0.0)
        rk = jnp.where(incl, gram[2 * c:, :2 * c], 0.0)
        ch["lpow"] = jnp.where(strict, gram[:2 * c, 2 * c:], 0.0)
        ch["rb"] = jnp.where(incl, gram[2 * c:, 2 * c:], 0.0).astype(BF16)
        ch["lv"] = _dot(jnp.concatenate([ak, rk], axis=0).astype(BF16), ch["v_s"])
    for ch in chains:
        ch["u"] = ch["xs"][:2 * c] + ch["lv"][:2 * c]
    for i in range(n_steps):
        for ch in chains:
            lb = ch["lpow"].astype(BF16)
            ch["u"] = ch["u"] + _dot(lb, ch["u"].astype(BF16))
            if i + 1 < n_steps:
                ch["lpow"] = _dot(lb, lb)
    for ch in chains:
        ub = ch["u"].astype(BF16)
        o_s = ch["xs"][2 * c:] + ch["lv"][2 * c:] + _dot(ch["rb"], ub)
        ch["o"] = o_s[:c] + o_s[c:]
        upd = _tn(jnp.concatenate([ch["v_s"], ub], axis=0), ch["suf"])
        s_ref[ch["bi"], ch["p"]] = ch["state"] * ch["g_tot"] + upd
    for ch in chains:
        bi, sl, o = ch["bi"], ch["sl"], ch["o"]
        mean = _segsum(o, bd) * (1.0 / HEAD)
        d = o - mean
        var = _segsum(d * d, bd) * (1.0 / HEAD)
        on = d * lax.rsqrt(var + GN_EPS) * gnw_ref[:, sl] + gnb_ref[:, sl]
        bonus = _segsum(ch["rkv"], bd) * ch["v"]
        o_ref[bi, :, sl] = ((on + bonus) * g_ref[bi, :, sl]).astype(o_ref.dtype)


def rwkv_scan(r, k, v, a, b, lw, g, batch, seq, r_k, gn_w, gn_b, bd128):
    m, w = r.shape
    nc = seq // CHUNK
    blk = lambda: pl.BlockSpec((batch, CHUNK, w), lambda ci: (0, ci, 0))
    vec = lambda: pl.BlockSpec((1, w), lambda ci: (0, 0))
    as3d = lambda x: x.reshape(batch, seq, w)
    out = pl.pallas_call(
        functools.partial(_rwkv_scan_kernel, batch=batch, width=w),
        grid=(nc,),
        in_specs=[blk() for _ in range(7)] + [vec(), vec(), vec(),
                                               pl.BlockSpec((LANES, LANES), lambda ci: (0, 0))],
        out_specs=blk(),
        out_shape=jax.ShapeDtypeStruct((batch, seq, w), BF16),
        scratch_shapes=[pltpu.VMEM((batch, w // LANES, LANES, LANES), F32)],
        compiler_params=_cparams(("arbitrary",)),
    )(as3d(r), as3d(k), as3d(v), as3d(a), as3d(b), as3d(lw), as3d(g),
      r_k.reshape(1, w), gn_w.reshape(1, w), gn_b.reshape(1, w), bd128)
    return out.reshape(m, w)


def _ffn_kernel(x_ref, g_ref, wg_ref, wu_ref, wd_ref, o_ref, h_ref, acc_ref):
    f = pl.program_id(1)

    @pl.when(f == 0)
    def _():
        h_ref[...] = _rms(x_ref[...], g_ref[...]).astype(BF16)
        acc_ref[...] = x_ref[...]

    h = h_ref[...]
    gate = _dot(h, wg_ref[...])
    up = _dot(h, wu_ref[...])
    act = (gate * _sigmoid(gate) * up).astype(BF16)
    acc_ref[...] += _dot(act, wd_ref[...])

    @pl.when(f == pl.num_programs(1) - 1)
    def _():
        o_ref[...] = acc_ref[...]


def ffn(x, g, wg, wu, wd, *, tm=1024, tf=256):
    m, d = x.shape
    f = wg.shape[1]
    tm, tf = _tile(m, tm), _tile(f, tf)
    return pl.pallas_call(
        _ffn_kernel,
        grid=(m // tm, f // tf),
        in_specs=[pl.BlockSpec((tm, d), lambda i, j: (i, 0)),
                  pl.BlockSpec((1, d), lambda i, j: (0, 0)),
                  pl.BlockSpec((d, tf), lambda i, j: (0, j)),
                  pl.BlockSpec((d, tf), lambda i, j: (0, j)),
                  pl.BlockSpec((tf, d), lambda i, j: (j, 0))],
        out_specs=pl.BlockSpec((tm, d), lambda i, j: (i, 0)),
        out_shape=jax.ShapeDtypeStruct((m, d), F32),
        scratch_shapes=[pltpu.VMEM((tm, d), BF16), pltpu.VMEM((tm, d), F32)],
        compiler_params=_cparams(("parallel", "arbitrary")),
    )(x, g.reshape(1, d), wg, wu, wd)


def _qkv_kernel(x_ref, g_ref, w_ref, pos_ref, gain_ref, freq_ref, bd_ref, o_ref,
                h_ref, cos_ref, sina_ref, sinb_ref, *, n_rot_tiles, tn):
    j = pl.program_id(1)

    @pl.when(j == 0)
    def _():
        h_ref[...] = _rms(x_ref[...], g_ref[...]).astype(BF16)
        ang = pos_ref[...] * freq_ref[...]
        seg = lax.broadcasted_iota(jnp.int32, ang.shape, 1) % HEAD
        half = ROPE_DIM // 2
        cos_ref[...] = jnp.where(seg < ROPE_DIM, jnp.cos(ang), 1.0)
        sin = jnp.sin(ang)
        sina_ref[...] = jnp.where(seg < half, -sin, 0.0)
        sinb_ref[...] = jnp.where((seg >= half) & (seg < ROPE_DIM), sin, 0.0)

    y = _dot(h_ref[...], w_ref[...])

    @pl.when(j < n_rot_tiles)
    def _():
        half = ROPE_DIM // 2
        for c in range(tn // LANES):
            sl = slice(c * LANES, (c + 1) * LANES)
            yc = y[:, sl]
            ms = _segsum(yc * yc, bd_ref[...]) * (1.0 / HEAD)
            yn = yc * lax.rsqrt(ms + RMS_EPS) * gain_ref[:, sl]
            rot = (yn * cos_ref[...] + pltpu.roll(yn, LANES - half, axis=1) * sina_ref[...]
                   + pltpu.roll(yn, half, axis=1) * sinb_ref[...])
            o_ref[:, sl] = rot.astype(o_ref.dtype)

    @pl.when(j >= n_rot_tiles)
    def _():
        o_ref[...] = y.astype(o_ref.dtype)


def qkv_project(x, g, w, pos, gains, freq, bd128, n_rot_cols, *, tm=1024, tn=512):
    m, d = x.shape
    n = w.shape[1]
    tm, tn = _tile(m, tm), _tile(n, tn)
    assert n_rot_cols % tn == 0
    return pl.pallas_call(
        functools.partial(_qkv_kernel, n_rot_tiles=n_rot_cols // tn, tn=tn),
        grid=(m // tm, n // tn),
        in_specs=[pl.BlockSpec((tm, d), lambda i, j: (i, 0)),
                  pl.BlockSpec((1, d), lambda i, j: (0, 0)),
                  pl.BlockSpec((d, tn), lambda i, j: (0, j)),
                  pl.BlockSpec((tm, 1), lambda i, j: (i, 0)),
                  pl.BlockSpec((1, tn), lambda i, j: (0, j)),
                  pl.BlockSpec((1, LANES), lambda i, j: (0, 0)),
                  pl.BlockSpec((LANES, LANES), lambda i, j: (0, 0))],
        out_specs=pl.BlockSpec((tm, tn), lambda i, j: (i, j)),
        out_shape=jax.ShapeDtypeStruct((m, n), BF16),
        scratch_shapes=[pltpu.VMEM((tm, d), BF16), pltpu.VMEM((tm, LANES), F32),
                        pltpu.VMEM((tm, LANES), F32), pltpu.VMEM((tm, LANES), F32)],
        compiler_params=_cparams(("parallel", "arbitrary")),
    )(x, g.reshape(1, d), w, pos, gains, freq, bd128)


def _attn_kernel(fast_ref, q_ref, k_ref, v_ref, lam_ref, subln_ref, o_ref, vt_ref, m_ref, ls_ref, l_ref,
                 acc_ref, *, tq, lambda_init):
    qi = pl.program_id(2)

    @pl.when(qi == 0)
    def _():
        for c in range(vt_ref.shape[0]):
            vt_ref[c] = v_ref[c * tq:(c + 1) * tq, :].astype(F32).T.astype(BF16)

    q = q_ref[...]
    lane = lax.broadcasted_iota(jnp.int32, q.shape, 1)
    zero = jnp.zeros_like(q)
    qs = jnp.concatenate([jnp.where(lane < HEAD, q, zero), jnp.where(lane < HEAD, zero, q)], axis=0)
    acc_ref[...] = jnp.zeros_like(acc_ref)

    def scores(j, masked):
        s = _nt(k_ref[pl.ds(pl.multiple_of(j * tq, tq), tq), :], qs)
        if masked:
            key = lax.broadcasted_iota(jnp.int32, (tq, tq), 0)
            qry = lax.broadcasted_iota(jnp.int32, (tq, tq), 1)
            keep = jnp.concatenate([key <= qry, key <= qry], axis=1)
            s = jnp.where(keep, s, NEG_BIG)
        return s

    def sweep(step):
        def body(j, carry):
            step(j, False)
            return carry
        lax.fori_loop(0, qi, body, 0)
        step(qi, True)

    @pl.when(fast_ref[0] == 1)
    def _():
        l_ref[...] = jnp.zeros_like(l_ref)

        def step(j, masked):
            p = jnp.exp2(scores(j, masked))
            l_ref[...] += jnp.sum(p.reshape(tq // SUBLANES, SUBLANES, 2 * tq), axis=0)
            acc_ref[...] += _dot(vt_ref[j], p.astype(BF16))

        sweep(step)
        ls_ref[...] = jnp.sum(l_ref[...], axis=0, keepdims=True)

    @pl.when(fast_ref[0] == 0)
    def _():
        m_ref[...] = jnp.full_like(m_ref, NEG_BIG)
        ls_ref[...] = jnp.zeros_like(ls_ref)

        def step(j, masked):
            s = scores(j, masked)
            m_old = m_ref[...]
            m_new = jnp.maximum(m_old, jnp.max(s, axis=0, keepdims=True))
            alpha = jnp.exp2(m_old - m_new)
            p = jnp.exp2(s - m_new)
            ls_ref[...] = alpha * ls_ref[...] + jnp.sum(p, axis=0, keepdims=True)
            acc_ref[...] = alpha * acc_ref[...] + _dot(vt_ref[j], p.astype(BF16))
            m_ref[...] = m_new

        sweep(step)

    lq = lam_ref[...]
    lam = (jnp.exp(jnp.sum(lq[0:1] * lq[1:2], axis=-1, keepdims=True))
           - jnp.exp(jnp.sum(lq[2:3] * lq[3:4], axis=-1, keepdims=True)) + lambda_init)
    o = acc_ref[...] / ls_ref[...]
    o = o[:, :tq] - lam * o[:, tq:]
    ms = jnp.mean(o * o, axis=0, keepdims=True)
    o = o * lax.rsqrt(ms + RMS_EPS) * subln_ref[...] * (1.0 - lambda_init)
    o_ref[...] = o.T.astype(o_ref.dtype)


def diff_attention(qkv, fast, batch, seq, n_heads, lam_params, subln, lambda_init, *, tq=512):
    m = qkv.shape[0]
    tq = _tile(seq, tq)
    nq = seq // tq
    grid_spec = pltpu.PrefetchScalarGridSpec(
        num_scalar_prefetch=1,
        grid=(batch, n_heads, nq),
        in_specs=[pl.BlockSpec((tq, LANES), lambda b, h, i, f: (b * nq + i, h)),
                  pl.BlockSpec((seq, LANES), lambda b, h, i, f: (b, n_heads + h)),
                  pl.BlockSpec((seq, LANES), lambda b, h, i, f: (b, 2 * n_heads + h)),
                  pl.BlockSpec((4, HEAD), lambda b, h, i, f: (0, 0)),
                  pl.BlockSpec((LANES, 1), lambda b, h, i, f: (0, 0))],
        out_specs=pl.BlockSpec((tq, LANES), lambda b, h, i, f: (b * nq + i, h)),
        scratch_shapes=[pltpu.VMEM((nq, LANES, tq), BF16),
                        pltpu.VMEM((1, 2 * tq), F32), pltpu.VMEM((1, 2 * tq), F32),
                        pltpu.VMEM((SUBLANES, 2 * tq), F32), pltpu.VMEM((LANES, 2 * tq), F32)])
    return pl.pallas_call(
        functools.partial(_attn_kernel, tq=tq, lambda_init=lambda_init),
        grid_spec=grid_spec,
        out_shape=jax.ShapeDtypeStruct((m, n_heads * LANES), BF16),
        compiler_params=_cparams(("parallel", "parallel", "arbitrary")),
    )(fast, qkv, qkv, qkv, lam_params, subln.reshape(LANES, 1))


def _router_kernel(x_ref, g_ref, router_ref, ids_ref, gates_ref):
    h = _rms(x_ref[...], g_ref[...])
    logits = _dot(h, router_ref[...], precision=HIGHEST)
    lane = lax.broadcasted_iota(jnp.int32, logits.shape, 1)
    logits = jnp.where(lane < N_EXPERTS, logits, NEG_BIG)
    v1 = jnp.max(logits, axis=-1, keepdims=True)
    i1 = jnp.min(jnp.where(logits == v1, lane, LANES), axis=-1, keepdims=True)
    rest = jnp.where(lane == i1, NEG_BIG, logits)
    v2 = jnp.max(rest, axis=-1, keepdims=True)
    i2 = jnp.min(jnp.where(rest == v2, lane, LANES), axis=-1, keepdims=True)
    e2 = jnp.exp(v2 - v1)
    ids_ref[...] = jnp.where(lane == 0, i1, i2)
    gates_ref[...] = jnp.where(lane == 0, 1.0 / (1.0 + e2), e2 / (1.0 + e2))


def moe_router(x, g, router_p, *, tm=1024):
    m, d = x.shape
    tm = _tile(m, tm)
    return pl.pallas_call(
        _router_kernel,
        grid=(m // tm,),
        in_specs=[pl.BlockSpec((tm, d), lambda i: (i, 0)),
                  pl.BlockSpec((1, d), lambda i: (0, 0)),
                  pl.BlockSpec((d, LANES), lambda i: (0, 0))],
        out_specs=[pl.BlockSpec((tm, LANES), lambda i: (i, 0))] * 2,
        out_shape=[jax.ShapeDtypeStruct((m, LANES), jnp.int32), jax.ShapeDtypeStruct((m, LANES), F32)],
        compiler_params=_cparams(("parallel",)),
    )(x, g.reshape(1, d), router_p)


def _route_tables(ids, tm, n_tiles):
    n_pairs = ids.shape[0] * 2
    n_rows = n_tiles * tm
    shift = max(n_pairs, n_rows).bit_length()
    e_flat = ids.reshape(-1)
    experts = jnp.arange(N_EXPERTS, dtype=jnp.int32)
    counts = jnp.sum((e_flat[:, None] == experts[None, :]).astype(jnp.int32), axis=0)
    padded = ((counts + tm - 1) // tm) * tm
    ends = jnp.cumsum(padded)
    pad_ends = jnp.cumsum(padded - counts)
    q = jnp.arange(n_rows - n_pairs, dtype=jnp.int32)
    pad_expert = jnp.sum((q[:, None] >= pad_ends[None, :]).astype(jnp.int32), axis=1)
    low = (1 << shift) - 1
    keys = jnp.concatenate([(e_flat << shift) | jnp.arange(n_pairs, dtype=jnp.int32),
                            (pad_expert << shift) | low])
    keys = jnp.sort(keys)
    perm = jnp.where((keys & low) == low, -1, keys & low)
    starts = jnp.arange(n_tiles, dtype=jnp.int32) * tm
    tile_expert = jnp.sum((starts[:, None] >= ends[None, :]).astype(jnp.int32), axis=1)
    n_valid = ends[-1] // tm
    last_expert = jnp.sum(jnp.where(jnp.arange(n_tiles) == n_valid - 1, tile_expert, 0))
    tile_expert = jnp.where(jnp.arange(n_tiles) < n_valid, tile_expert, last_expert)
    n_real = jnp.sum((perm >= 0).reshape(n_tiles, tm).astype(jnp.int32), axis=1)
    return perm, tile_expert.astype(jnp.int32), n_valid.reshape(1).astype(jnp.int32), n_real


def _moe_group_kernel(te_ref, nv_ref, perm_ref, nr_ref, x_hbm, g_ref, wg_ref, wu_ref, wd_ref, y_hbm,
                      xbuf, ybuf, h_ref, acc_ref, gsem, ssem, *, tm, n_tokens):
    i = pl.program_id(0)
    f = pl.program_id(1)
    valid = i < nv_ref[0]
    base = i * tm
    d = h_ref.shape[1]
    n_chunks = d // LANES

    def tile_rows(k):
        return pl.ds(pl.multiple_of(k * SUBLANES, SUBLANES), SUBLANES)

    slot = i % 2

    def tile_in(s, r, t):
        return pltpu.make_async_copy(x_hbm.at[tile_rows(t), :], xbuf.at[s, tile_rows(r), :], gsem.at[s])

    def tile_out(r, dst):
        return pltpu.make_async_copy(ybuf.at[tile_rows(r), :], y_hbm.at[tile_rows(dst), :], ssem)

    def chunk_rows(c):
        return pl.ds(c, tm, stride=SUBLANES)

    def start_gather(tile, s):
        def start(r, carry):
            j = perm_ref[tile * tm + r]
            tile_in(s, r, jnp.maximum(j, 0) >> 1).start()
            return carry
        lax.fori_loop(0, tm, start, 0, unroll=8)

    def wait_scatter(count):
        def wait(r, carry):
            tile_out(r, 0).wait()
            return carry
        lax.fori_loop(0, count, wait, 0)

    @pl.when(valid & (f == 0))
    def _():
        @pl.when(i == 0)
        def _():
            start_gather(0, 0)

        def wait(r, carry):
            tile_in(slot, r, 0).wait()
            return carry
        lax.fori_loop(0, tm, wait, 0, unroll=8)

        @pl.when(i + 1 < nv_ref[0])
        def _():
            start_gather(i + 1, 1 - slot)

        ss = jnp.zeros((tm, 1), F32)
        for c in range(n_chunks):
            xc = xbuf[slot, chunk_rows(c), :]
            ss = ss + jnp.sum(xc * xc, axis=-1, keepdims=True)
        inv = lax.rsqrt(ss * (1.0 / d) + RMS_EPS)
        for c in range(n_chunks):
            sl = slice(c * LANES, (c + 1) * LANES)
            h_ref[:, sl] = (xbuf[slot, chunk_rows(c), :] * inv * g_ref[:, sl]).astype(BF16)
        acc_ref[...] = jnp.zeros_like(acc_ref)

    @pl.when(valid)
    def _():
        h = h_ref[...]
        gate = _dot(h, wg_ref[...])
        up = _dot(h, wu_ref[...])
        act = (gate * _sigmoid(gate) * up).astype(BF16)
        acc_ref[...] += _dot(act, wd_ref[...])

    @pl.when(valid & (f == pl.num_programs(1) - 1))
    def _():
        @pl.when(i > 0)
        def _():
            wait_scatter(nr_ref[jnp.maximum(i - 1, 0)])

        for c in range(n_chunks):
            ybuf[chunk_rows(c), :] = acc_ref[:, c * LANES:(c + 1) * LANES]

        n_real = nr_ref[i]

        def start(r):
            j = perm_ref[base + r]
            tile_out(r, (j & 1) * n_tokens + (j >> 1)).start()

        def start8(r8, carry):
            for k in range(SUBLANES):
                start(r8 * SUBLANES + k)
            return carry
        lax.fori_loop(0, n_real // SUBLANES, start8, 0)

        def start1(r, carry):
            start(r)
            return carry
        lax.fori_loop((n_real // SUBLANES) * SUBLANES, n_real, start1, 0)

        @pl.when(i == nv_ref[0] - 1)
        def _():
            wait_scatter(n_real)


def moe_experts(x8, g, perm, tile_expert, n_valid, n_real, wg, wu, wd, *, tm, tf=512):
    d = g.shape[0]
    m = x8.shape[0] // SUBLANES
    f = wg.shape[2]
    tf = _tile(f, tf)
    n_tiles, n_f = tile_expert.shape[0], f // tf
    fidx = lambda i, j, nv: jnp.where(i < nv[0], j, n_f - 1)
    grid_spec = pltpu.PrefetchScalarGridSpec(
        num_scalar_prefetch=4,
        grid=(n_tiles, n_f),
        in_specs=[pl.BlockSpec(memory_space=pl.ANY),
                  pl.BlockSpec((1, d), lambda i, j, te, nv, pm, nr: (0, 0)),
                  pl.BlockSpec((None, d, tf), lambda i, j, te, nv, pm, nr: (te[i], 0, fidx(i, j, nv))),
                  pl.BlockSpec((None, d, tf), lambda i, j, te, nv, pm, nr: (te[i], 0, fidx(i, j, nv))),
                  pl.BlockSpec((None, tf, d), lambda i, j, te, nv, pm, nr: (te[i], fidx(i, j, nv), 0))],
        out_specs=pl.BlockSpec(memory_space=pl.ANY),
        scratch_shapes=[pltpu.VMEM((2, tm * SUBLANES, LANES), F32), pltpu.VMEM((tm * SUBLANES, LANES), F32),
                        pltpu.VMEM((tm, d), BF16), pltpu.VMEM((tm, d), F32),
                        pltpu.SemaphoreType.DMA((2,)), pltpu.SemaphoreType.DMA(())])
    return pl.pallas_call(
        functools.partial(_moe_group_kernel, tm=tm, n_tokens=m),
        grid_spec=grid_spec,
        out_shape=jax.ShapeDtypeStruct((2 * m * SUBLANES, LANES), F32),
        compiler_params=_cparams(("arbitrary", "arbitrary")),
    )(tile_expert, n_valid, perm, n_real, x8, g.reshape(1, d), wg, wu, wd)


def _moe_combine_kernel(x_ref, y0_ref, y1_ref, gates_ref, o_ref, *, tm):
    gates = gates_ref[...]
    g0, g1 = gates[:, 0:1], gates[:, 1:2]
    for c in range(x_ref.shape[1] // LANES):
        sl = slice(c * LANES, (c + 1) * LANES)
        rows = pl.ds(c, tm, stride=SUBLANES)
        o_ref[:, sl] = x_ref[:, sl] + g0 * y0_ref[rows, :] + g1 * y1_ref[rows, :]


def moe_combine(x, y8, gates, *, tm=1024):
    m, d = x.shape
    tm = _tile(m, tm)
    nb = m // tm
    return pl.pallas_call(
        functools.partial(_moe_combine_kernel, tm=tm),
        grid=(nb,),
        in_specs=[pl.BlockSpec((tm, d), lambda i: (i, 0)),
                  pl.BlockSpec((tm * SUBLANES, LANES), lambda i: (i, 0)),
                  pl.BlockSpec((tm * SUBLANES, LANES), lambda i: (nb + i, 0)),
                  pl.BlockSpec((tm, LANES), lambda i: (i, 0))],
        out_specs=pl.BlockSpec((tm, d), lambda i: (i, 0)),
        out_shape=jax.ShapeDtypeStruct((m, d), F32),
        compiler_params=_cparams(("parallel",)),
    )(x, y8, y8, gates)


def moe(x, g, router_p, wg, wu, wd, *, tm=1024):
    m, d = x.shape
    assert d == SUBLANES * LANES
    tm = _tile(m, tm)
    n_tiles = (2 * m) // tm + N_EXPERTS
    ids, gates = moe_router(x, g, router_p)
    perm, tile_expert, n_valid, n_real = _route_tables(ids[:, :2], tm, n_tiles)
    x8 = x.reshape(m * SUBLANES, LANES)
    y8 = moe_experts(x8, g, perm, tile_expert, n_valid, n_real, wg, wu, wd, tm=tm)
    return moe_combine(x, y8, gates, tm=tm)


def _block_diag(blocks):
    n, h, _ = blocks.shape
    eye = jnp.eye(n, dtype=blocks.dtype)
    return (eye[:, None, :, None] * blocks[:, :, None, :]).reshape(n * h, n * h)


def _seg_ones(n):
    seg = jnp.arange(n) // HEAD
    return (seg[:, None] == seg[None, :]).astype(BF16)


def _even_layer(x, batch, seq, ln_mix, w_in, conv_w, conv_b, gate_a_w, gate_a_b, gate_x_w, gate_x_b,
                lru_lambda, shift_mu, w0, w2, a0, a2, g2, k_k, k_a, r_k, gn_w, gn_b, w_out,
                ln_ffn, ffn_gate, ffn_up, ffn_down):
    lru_w = lru_lambda.shape[0]
    rw_w = w0.shape[0]
    dl, al, gl = w2.shape[0], a2.shape[0], g2.shape[0]
    n_in = w_in.shape[1]
    proj = norm_matmul(x, ln_mix, w_in.astype(BF16), tn=n_in // 2)

    y_lru = lru_branch(proj, batch, seq, conv_w, conv_b,
                       _block_diag(gate_a_w).astype(BF16), gate_a_b,
                       _block_diag(gate_x_w).astype(BF16), gate_x_b, lru_lambda)

    lora = dl + al + gl
    zeros = lambda n: jnp.zeros((n, rw_w), F32)
    w2p = jnp.concatenate([w2, zeros(al + gl)], axis=0).astype(BF16)
    a2p = jnp.concatenate([zeros(dl), a2, zeros(gl)], axis=0).astype(BF16)
    g2p = jnp.concatenate([zeros(dl + al), g2], axis=0).astype(BF16)
    assert lora == w2p.shape[0]
    r, k, v, a, b, lw, g = rwkv_prep(proj, seq, 2 * lru_w, shift_mu, w0, w2p, a0, a2p, g2p, k_k, k_a,
                                     _seg_ones(rw_w))
    y_rwkv = rwkv_scan(r, k, v, a, b, lw, g, batch, seq, r_k.reshape(-1), gn_w, gn_b, _seg_ones(LANES))

    w_out = w_out.astype(BF16)
    x = matmul_residual([y_lru, y_rwkv], [w_out[:lru_w], w_out[lru_w:]], x)
    return ffn(x, ln_ffn, ffn_gate.astype(BF16), ffn_up.astype(BF16), ffn_down.astype(BF16))


def _odd_layer(x, pos, batch, seq, layer_idx, ln_mix, w_qkv, q_norm, k_norm, lq1, lk1, lq2, lk2, subln,
               w_o, ln_ffn, router, moe_gate, moe_up, moe_down):
    d = x.shape[1]
    n_heads = d // (2 * HEAD)
    qd = n_heads * 2 * HEAD
    lambda_init = 0.8 - 0.6 * math.exp(-0.3 * layer_idx)
    reps = qd // HEAD
    q_gain = q_norm * (HEAD ** -0.5 * LOG2E)
    logit_bound = 1.02 * HEAD * jnp.max(jnp.abs(q_gain)) * jnp.max(jnp.abs(k_norm))
    fast = (logit_bound <= MAX_EXP2_ARG).astype(jnp.int32).reshape(1)
    gains = jnp.concatenate([jnp.tile(q_gain, reps),
                             jnp.tile(k_norm, reps),
                             jnp.ones((w_qkv.shape[1] - 2 * qd,), F32)])[None, :]
    seg = jnp.arange(LANES) % HEAD
    inv_freq = ROPE_THETA ** (-(2.0 * (seg % (ROPE_DIM // 2))).astype(F32) / ROPE_DIM)
    freq = jnp.where(seg < ROPE_DIM, inv_freq, 0.0)[None, :].astype(F32)
    qkv = qkv_project(x, ln_mix, w_qkv.astype(BF16), pos, gains, freq, _seg_ones(LANES), 2 * qd)
    lam_params = jnp.stack([lq1, lk1, lq2, lk2]).astype(F32)
    o = diff_attention(qkv, fast, batch, seq, n_heads, lam_params, subln, lambda_init)
    x = matmul_residual([o], [w_o.astype(BF16)], x)
    router_p = jnp.pad(router, ((0, 0), (0, LANES - router.shape[1])))
    return moe(x, ln_ffn, router_p, moe_gate.astype(BF16), moe_up.astype(BF16), moe_down.astype(BF16))


def kernel(x, positions, e_ln_mix, e_w_in, e_conv_w, e_conv_b, e_gate_a_w, e_gate_a_b, e_gate_x_w, e_gate_x_b, e_lru_lambda, e_shift_mu, e_w0, e_w2, e_a0, e_a2, e_g2, e_k_k, e_k_a, e_r_k, e_gn_w, e_gn_b, e_w_out, e_ln_ffn, e_ffn_gate, e_ffn_up, e_ffn_down, o_ln_mix, o_w_qkv, o_q_norm, o_k_norm, o_lambda_q1, o_lambda_k1, o_lambda_q2, o_lambda_k2, o_subln, o_w_o, o_ln_ffn, o_router, o_moe_gate, o_moe_up, o_moe_down):
    batch, seq, d = x.shape
    depth = e_ln_mix.shape[0] + o_ln_mix.shape[0]
    xf = x.reshape(batch * seq, d)
    pos = positions.reshape(batch * seq, 1).astype(F32)
    for i in range(depth):
        j = i // 2
        if i % 2 == 0:
            xf = _even_layer(xf, batch, seq, e_ln_mix[j], e_w_in[j], e_conv_w[j], e_conv_b[j],
                             e_gate_a_w[j], e_gate_a_b[j], e_gate_x_w[j], e_gate_x_b[j],
                             e_lru_lambda[j], e_shift_mu[j], e_w0[j], e_w2[j], e_a0[j], e_a2[j],
                             e_g2[j], e_k_k[j], e_k_a[j], e_r_k[j], e_gn_w[j], e_gn_b[j], e_w_out[j],
                             e_ln_ffn[j], e_ffn_gate[j], e_ffn_up[j], e_ffn_down[j])
        else:
            xf = _odd_layer(xf, pos, batch, seq, i, o_ln_mix[j], o_w_qkv[j], o_q_norm[j], o_k_norm[j],
                            o_lambda_q1[j], o_lambda_k1[j], o_lambda_q2[j], o_lambda_k2[j], o_subln[j],
                            o_w_o[j], o_ln_ffn[j], o_router[j], o_moe_gate[j], o_moe_up[j],
                            o_moe_down[j])
    return xf.reshape(batch, seq, d)
```

```python
import functools
import math

import jax
import jax.numpy as jnp
from jax import lax
from jax.experimental import pallas as pl
from jax.experimental.pallas import tpu as pltpu

F32 = jnp.float32
BF16 = jnp.bfloat16
HIGHEST = lax.Precision.HIGHEST

LANES = 128
SUBLANES = 8
VMEM_LIMIT = 56 * 1024 * 1024

HEAD = 64
CONV_WIDTH = 4
LRU_C = 8.0
GN_EPS = 64e-5
RMS_EPS = 1e-6
ROPE_DIM = 16
ROPE_THETA = 500000.0
N_EXPERTS = 8
CHUNK = 64
NEG_BIG = -1e30
LOG2E = 1.4426950408889634
MAX_EXP2_ARG = 60.0


def _cparams(sem):
    return pltpu.CompilerParams(dimension_semantics=sem, vmem_limit_bytes=VMEM_LIMIT)


def _tile(n, pref):
    t = min(n, pref)
    assert n % t == 0, (n, pref)
    return t


def _nt(a, b, **kw):
    return lax.dot_general(a, b, (((1,), (1,)), ((), ())), preferred_element_type=F32, **kw)


def _tn(a, b, **kw):
    return lax.dot_general(a, b, (((0,), (0,)), ((), ())), preferred_element_type=F32, **kw)


def _dot(a, b, **kw):
    return jnp.dot(a, b, preferred_element_type=F32, **kw)


def _segsum(x, bd):
    hi = x.astype(BF16)
    lo = (x - hi.astype(F32)).astype(BF16)
    return _dot(hi, bd) + _dot(lo, bd)


def _rms(x, g):
    ms = jnp.mean(x * x, axis=-1, keepdims=True)
    return x * lax.rsqrt(ms + RMS_EPS) * g


def _sigmoid(x):
    return 1.0 / (1.0 + jnp.exp(-x))


def _softplus(x):
    return jnp.maximum(x, 0.0) + jnp.log1p(jnp.exp(-jnp.abs(x)))


def _norm_mm_kernel(x_ref, g_ref, w_ref, o_ref, h_ref):
    @pl.when(pl.program_id(1) == 0)
    def _():
        h_ref[...] = _rms(x_ref[...], g_ref[...]).astype(BF16)

    o_ref[...] = _dot(h_ref[...], w_ref[...]).astype(o_ref.dtype)


def norm_matmul(x, g, w, *, tm=1024, tn=512, out_dtype=F32):
    m, d = x.shape
    n = w.shape[1]
    tm, tn = _tile(m, tm), _tile(n, tn)
    return pl.pallas_call(
        _norm_mm_kernel,
        grid=(m // tm, n // tn),
        in_specs=[pl.BlockSpec((tm, d), lambda i, j: (i, 0)),
                  pl.BlockSpec((1, d), lambda i, j: (0, 0)),
                  pl.BlockSpec((d, tn), lambda i, j: (0, j))],
        out_specs=pl.BlockSpec((tm, tn), lambda i, j: (i, j)),
        out_shape=jax.ShapeDtypeStruct((m, n), out_dtype),
        scratch_shapes=[pltpu.VMEM((tm, d), BF16)],
        compiler_params=_cparams(("parallel", "arbitrary")),
    )(x, g.reshape(1, d), w)


def _mm_res_kernel(*refs, n_in):
    ys, ws = refs[:n_in], refs[n_in:2 * n_in]
    res_ref, o_ref = refs[2 * n_in], refs[2 * n_in + 1]
    acc = res_ref[...]
    for y_ref, w_ref in zip(ys, ws):
        acc = acc + _dot(y_ref[...], w_ref[...])
    o_ref[...] = acc


def matmul_residual(ys, ws, res, *, tm=1024, tn=512):
    m, n = res.shape
    tm, tn = _tile(m, tm), _tile(n, tn)
    n_in = len(ys)
    in_specs = [pl.BlockSpec((tm, y.shape[1]), lambda i, j: (i, 0)) for y in ys]
    in_specs += [pl.BlockSpec((w.shape[0], tn), lambda i, j: (0, j)) for w in ws]
    in_specs += [pl.BlockSpec((tm, tn), lambda i, j: (i, j))]
    return pl.pallas_call(
        functools.partial(_mm_res_kernel, n_in=n_in),
        grid=(m // tm, n // tn),
        in_specs=in_specs,
        out_specs=pl.BlockSpec((tm, tn), lambda i, j: (i, j)),
        out_shape=jax.ShapeDtypeStruct((m, n), F32),
        compiler_params=_cparams(("parallel", "arbitrary")),
    )(*ys, *ws, res)


def _lru_kernel(x_ref, gate_ref, cw_ref, cb_ref, wa_ref, ba_ref, wx_ref, bx_ref, lam_ref,
                o_ref, tail_ref, h_ref, *, tm):
    @pl.when(pl.program_id(1) == 0)
    def _():
        tail_ref[...] = jnp.zeros_like(tail_ref)
        h_ref[...] = jnp.zeros_like(h_ref)

    x = x_ref[...]
    xx = jnp.concatenate([tail_ref[...], x], axis=0)
    tail_ref[...] = x[tm - 8:, :]
    cw = cw_ref[...]
    xc = cb_ref[...] + cw[CONV_WIDTH - 1:CONV_WIDTH, :] * x
    for s in range(1, CONV_WIDTH):
        xc = xc + cw[CONV_WIDTH - 1 - s:CONV_WIDTH - s, :] * pltpu.roll(xx, s, axis=0)[8:, :]

    xb = xc.astype(BF16)
    gate_a = _dot(xb, wa_ref[...]) + ba_ref[...]
    gate_x = _dot(xb, wx_ref[...]) + bx_ref[...]
    log_a = -LRU_C * _sigmoid(gate_a) * _softplus(-lam_ref[...])
    a = jnp.exp(log_a)
    u = jnp.sqrt(1.0 - a * a) * _sigmoid(gate_x) * xc

    row = lax.broadcasted_iota(jnp.int32, a.shape, 0)
    s = 1
    while s < tm:
        keep = row >= s
        a_s = jnp.where(keep, pltpu.roll(a, s, axis=0), 1.0)
        u_s = jnp.where(keep, pltpu.roll(u, s, axis=0), 0.0)
        u = a * u_s + u
        a = a * a_s
        s *= 2
    h = a * h_ref[...] + u
    h_ref[...] = h[tm - 1:tm, :]
    o_ref[...] = (jax.nn.gelu(gate_ref[...]) * h).astype(o_ref.dtype)


def lru_branch(proj, batch, seq, conv_w, conv_b, wa_bd, ba, wx_bd, bx, lam, *, tm=256):
    m = proj.shape[0]
    w = lam.shape[0]
    tm = _tile(seq, tm)
    nt = seq // tm
    vec = lambda: pl.BlockSpec((1, w), lambda b, i: (0, 0))
    return pl.pallas_call(
        functools.partial(_lru_kernel, tm=tm),
        grid=(batch, nt),
        in_specs=[pl.BlockSpec((tm, w), lambda b, i: (b * nt + i, 0)),
                  pl.BlockSpec((tm, w), lambda b, i: (b * nt + i, 1)),
                  pl.BlockSpec((CONV_WIDTH, w), lambda b, i: (0, 0)),
                  vec(),
                  pl.BlockSpec((w, w), lambda b, i: (0, 0)), vec(),
                  pl.BlockSpec((w, w), lambda b, i: (0, 0)), vec(),
                  vec()],
        out_specs=pl.BlockSpec((tm, w), lambda b, i: (b * nt + i, 0)),
        out_shape=jax.ShapeDtypeStruct((m, w), BF16),
        scratch_shapes=[pltpu.VMEM((8, w), F32), pltpu.VMEM((1, w), F32)],
        compiler_params=_cparams(("parallel", "arbitrary")),
    )(proj, proj, conv_w, conv_b.reshape(1, w), wa_bd, ba.reshape(1, w), wx_bd, bx.reshape(1, w),
      lam.reshape(1, w))


def _rwkv_prep_kernel(pr_ref, pk_ref, pv_ref, pl_ref, qr_ref, qk_ref, qv_ref, ql_ref,
                      mur_ref, muk_ref, muv_ref, mul_ref, w0_ref, w2_ref, a0_ref, a2_ref, g2_ref,
                      kk_ref, ka_ref, bd_ref,
                      r_out, k_out, v_out, a_out, b_out, lw_out, g_out, *, tm, seq):
    first = (pl.program_id(0) * tm) % seq == 0

    def shift_mix(p_ref, q_ref, mu_ref):
        x = p_ref[...]
        prev = jnp.where(first, 0.0, q_ref[7:8, :])
        row = lax.broadcasted_iota(jnp.int32, x.shape, 0)
        xs = jnp.where(row == 0, prev, pltpu.roll(x, 1, axis=0))
        return x + (xs - x) * mu_ref[...]

    r = shift_mix(pr_ref, qr_ref, mur_ref)
    k = shift_mix(pk_ref, qk_ref, muk_ref)
    v = shift_mix(pv_ref, qv_ref, muv_ref)
    lo = shift_mix(pl_ref, ql_ref, mul_ref)

    wlog = -_softplus(-(w0_ref[...] + _dot(jnp.tanh(lo).astype(BF16), w2_ref[...]))) - 0.5
    a = _sigmoid(a0_ref[...] + _dot(lo.astype(BF16), a2_ref[...]))
    g = _dot(_sigmoid(lo).astype(BF16), g2_ref[...])

    kk = k * kk_ref[...]
    nrm = jnp.sqrt(_segsum(kk * kk, bd_ref[...]))
    kk = kk / jnp.maximum(nrm, 1e-12)

    r_out[...] = r
    k_out[...] = k * (1.0 + (a - 1.0) * ka_ref[...])
    v_out[...] = v
    a_out[...] = -kk
    b_out[...] = kk * a
    lw_out[...] = -jnp.exp(wlog)
    g_out[...] = g


def rwkv_prep(proj, seq, col0, mu, w0, w2p, a0, a2p, g2p, k_k, k_a, bd, *, tm=512):
    m = proj.shape[0]
    w = w0.shape[0]
    lw = w2p.shape[0]
    tm = _tile(seq, tm)
    cb = col0 // w
    lb = (col0 + 3 * w) // lw
    main = lambda c, width: pl.BlockSpec((tm, width), lambda i: (i, c))
    prev = lambda c, width: pl.BlockSpec((8, width), lambda i: (jnp.maximum(i * (tm // 8) - 1, 0), c))
    vec = lambda width: pl.BlockSpec((1, width), lambda i: (0, 0))
    mat = lambda a: pl.BlockSpec(a.shape, lambda i: (0, 0))
    mu_r, mu_k, mu_v, mu_l = (mu[None, 0:w], mu[None, w:2 * w], mu[None, 2 * w:3 * w], mu[None, 3 * w:])
    outs = [jax.ShapeDtypeStruct((m, w), F32)] * 7
    return pl.pallas_call(
        functools.partial(_rwkv_prep_kernel, tm=tm, seq=seq),
        grid=(m // tm,),
        in_specs=[main(cb, w), main(cb + 1, w), main(cb + 2, w), main(lb, lw),
                  prev(cb, w), prev(cb + 1, w), prev(cb + 2, w), prev(lb, lw),
                  vec(w), vec(w), vec(w), vec(lw),
                  vec(w), mat(w2p), vec(w), mat(a2p), mat(g2p), vec(w), vec(w), mat(bd)],
        out_specs=[pl.BlockSpec((tm, w), lambda i: (i, 0))] * 7,
        out_shape=outs,
        compiler_params=_cparams(("parallel",)),
    )(proj, proj, proj, proj, proj, proj, proj, proj, mu_r, mu_k, mu_v, mu_l,
      w0.reshape(1, w), w2p, a0.reshape(1, w), a2p, g2p, k_k.reshape(1, w), k_a.reshape(1, w), bd)


def _rwkv_scan_kernel(r_ref, k_ref, v_ref, a_ref, b_ref, lw_ref, g_ref, rk_ref, gnw_ref, gnb_ref,
                      bd_ref, o_ref, s_ref, *, batch, width):
    @pl.when(pl.program_id(0) == 0)
    def _():
        s_ref[...] = jnp.zeros_like(s_ref)

    c = CHUNK
    tri = (lax.broadcasted_iota(jnp.int32, (c, c), 0) >= lax.broadcasted_iota(jnp.int32, (c, c), 1))
    lane = lax.broadcasted_iota(jnp.int32, (c, LANES), 1)
    head0 = lane < HEAD
    i2 = lax.broadcasted_iota(jnp.int32, (2 * c, 2 * c), 0)
    j2 = lax.broadcasted_iota(jnp.int32, (2 * c, 2 * c), 1)
    strict = i2 > j2
    incl = i2 >= j2
    bd = bd_ref[...]
    n_steps = int(math.log2(c))

    def stack(x):
        xb = x.astype(BF16)
        zero = jnp.zeros_like(xb)
        return jnp.concatenate([jnp.where(head0, xb, zero), jnp.where(head0, zero, xb)], axis=0)

    chains = []
    for bi in range(batch):
        lw = lw_ref[bi]
        cum = _dot(tri.astype(F32), lw, precision=HIGHEST)
        tot = cum[c - 1:c, :]
        g_out = jnp.exp(-cum)
        g_suf = jnp.exp(tot - cum)
        g_tot = jnp.exp(tot)
        r, k, v, b = r_ref[bi], k_ref[bi], v_ref[bi], b_ref[bi]
        rt = r * jnp.exp(cum)
        kt = k * g_out
        at = a_ref[bi] * jnp.exp(cum - lw)
        bt = b * g_out
        ks = k * g_suf
        bs = b * g_suf
        rkv = r * k * rk_ref[...]
        for p in range(width // LANES):
            sl = slice(p * LANES, (p + 1) * LANES)
            chains.append(dict(
                bi=bi, p=p, sl=sl, v=v[:, sl], rkv=rkv[:, sl], g_tot=g_tot[:, sl],
                v_s=stack(v[:, sl]),
                ar=jnp.concatenate([stack(at[:, sl]), stack(rt[:, sl])], axis=0),
                kb=jnp.concatenate([stack(kt[:, sl]), stack(bt[:, sl])], axis=0),
                suf=jnp.concatenate([stack(ks[:, sl]), stack(bs[:, sl])], axis=0)))

    for ch in chains:
        ch["state"] = s_ref[ch["bi"], ch["p"]]
        ch["gram"] = _nt(ch["ar"], ch["kb"])
        ch["xs"] = _nt(ch["ar"], ch["state"].astype(BF16))
    for ch in chains:
        gram = ch["gram"]
        ak = jnp.where(strict, gram[:2 * c, :2 * c], 0.0)
        rk = jnp.where(incl, gram[2 * c:, :2 * c], 0.0)
        ch["lpow"] = jnp.where(strict, gram[:2 * c, 2 * c:], 0.0)
        ch["rb"] = jnp.where(incl, gram[2 * c:, 2 * c:], 0.0).astype(BF16)
        ch["lv"] = _dot(jnp.concatenate([ak, rk], axis=0).astype(BF16), ch["v_s"])
    for ch in chains:
        ch["u"] = ch["xs"][:2 * c] + ch["lv"][:2 * c]
    for i in range(n_steps):
        for ch in chains:
            lb = ch["lpow"].astype(BF16)
            ch["u"] = ch["u"] + _dot(lb, ch["u"].astype(BF16))
            if i + 1 < n_steps:
                ch["lpow"] = _dot(lb, lb)
    for ch in chains:
        ub = ch["u"].astype(BF16)
        o_s = ch["xs"][2 * c:] + ch["lv"][2 * c:] + _dot(ch["rb"], ub)
        ch["o"] = o_s[:c] + o_s[c:]
        upd = _tn(jnp.concatenate([ch["v_s"], ub], axis=0), ch["suf"])
        s_ref[ch["bi"], ch["p"]] = ch["state"] * ch["g_tot"] + upd
    for ch in chains:
        bi, sl, o = ch["bi"], ch["sl"], ch["o"]
        mean = _segsum(o, bd) * (1.0 / HEAD)
        d = o - mean
        var = _segsum(d * d, bd) * (1.0 / HEAD)
        on = d * lax.rsqrt(var + GN_EPS) * gnw_ref[:, sl] + gnb_ref[:, sl]
        bonus = _segsum(ch["rkv"], bd) * ch["v"]
        o_ref[bi, :, sl] = ((on + bonus) * g_ref[bi, :, sl]).astype(o_ref.dtype)


def rwkv_scan(r, k, v, a, b, lw, g, batch, seq, r_k, gn_w, gn_b, bd128):
    m, w = r.shape
    nc = seq // CHUNK
    blk = lambda: pl.BlockSpec((batch, CHUNK, w), lambda ci: (0, ci, 0))
    vec = lambda: pl.BlockSpec((1, w), lambda ci: (0, 0))
    as3d = lambda x: x.reshape(batch, seq, w)
    out = pl.pallas_call(
        functools.partial(_rwkv_scan_kernel, batch=batch, width=w),
        grid=(nc,),
        in_specs=[blk() for _ in range(7)] + [vec(), vec(), vec(),
                                               pl.BlockSpec((LANES, LANES), lambda ci: (0, 0))],
        out_specs=blk(),
        out_shape=jax.ShapeDtypeStruct((batch, seq, w), BF16),
        scratch_shapes=[pltpu.VMEM((batch, w // LANES, LANES, LANES), F32)],
        compiler_params=_cparams(("arbitrary",)),
    )(as3d(r), as3d(k), as3d(v), as3d(a), as3d(b), as3d(lw), as3d(g),
      r_k.reshape(1, w), gn_w.reshape(1, w), gn_b.reshape(1, w), bd128)
    return out.reshape(m, w)


def _ffn_kernel(x_ref, g_ref, wg_ref, wu_ref, wd_ref, o_ref, h_ref, acc_ref):
    f = pl.program_id(1)

    @pl.when(f == 0)
    def _():
        h_ref[...] = _rms(x_ref[...], g_ref[...]).astype(BF16)
        acc_ref[...] = x_ref[...]

    h = h_ref[...]
    gate = _dot(h, wg_ref[...])
    up = _dot(h, wu_ref[...])
    act = (gate * _sigmoid(gate) * up).astype(BF16)
    acc_ref[...] += _dot(act, wd_ref[...])

    @pl.when(f == pl.num_programs(1) - 1)
    def _():
        o_ref[...] = acc_ref[...]


def ffn(x, g, wg, wu, wd, *, tm=1024, tf=256):
    m, d = x.shape
    f = wg.shape[1]
    tm, tf = _tile(m, tm), _tile(f, tf)
    return pl.pallas_call(
        _ffn_kernel,
        grid=(m // tm, f // tf),
        in_specs=[pl.BlockSpec((tm, d), lambda i, j: (i, 0)),
                  pl.BlockSpec((1, d), lambda i, j: (0, 0)),
                  pl.BlockSpec((d, tf), lambda i, j: (0, j)),
                  pl.BlockSpec((d, tf), lambda i, j: (0, j)),
                  pl.BlockSpec((tf, d), lambda i, j: (j, 0))],
        out_specs=pl.BlockSpec((tm, d), lambda i, j: (i, 0)),
        out_shape=jax.ShapeDtypeStruct((m, d), F32),
        scratch_shapes=[pltpu.VMEM((tm, d), BF16), pltpu.VMEM((tm, d), F32)],
        compiler_params=_cparams(("parallel", "arbitrary")),
    )(x, g.reshape(1, d), wg, wu, wd)


def _qkv_kernel(x_ref, g_ref, w_ref, pos_ref, gain_ref, freq_ref, bd_ref, o_ref,
                h_ref, cos_ref, sina_ref, sinb_ref, *, n_rot_tiles, tn):
    j = pl.program_id(1)

    @pl.when(j == 0)
    def _():
        h_ref[...] = _rms(x_ref[...], g_ref[...]).astype(BF16)
        ang = pos_ref[...] * freq_ref[...]
        seg = lax.broadcasted_iota(jnp.int32, ang.shape, 1) % HEAD
        half = ROPE_DIM // 2
        cos_ref[...] = jnp.where(seg < ROPE_DIM, jnp.cos(ang), 1.0)
        sin = jnp.sin(ang)
        sina_ref[...] = jnp.where(seg < half, -sin, 0.0)
        sinb_ref[...] = jnp.where((seg >= half) & (seg < ROPE_DIM), sin, 0.0)

    y = _dot(h_ref[...], w_ref[...])

    @pl.when(j < n_rot_tiles)
    def _():
        half = ROPE_DIM // 2
        for c in range(tn // LANES):
            sl = slice(c * LANES, (c + 1) * LANES)
            yc = y[:, sl]
            ms = _segsum(yc * yc, bd_ref[...]) * (1.0 / HEAD)
            yn = yc * lax.rsqrt(ms + RMS_EPS) * gain_ref[:, sl]
            rot = (yn * cos_ref[...] + pltpu.roll(yn, LANES - half, axis=1) * sina_ref[...]
                   + pltpu.roll(yn, half, axis=1) * sinb_ref[...])
            o_ref[:, sl] = rot.astype(o_ref.dtype)

    @pl.when(j >= n_rot_tiles)
    def _():
        o_ref[...] = y.astype(o_ref.dtype)


def qkv_project(x, g, w, pos, gains, freq, bd128, n_rot_cols, *, tm=1024, tn=512):
    m, d = x.shape
    n = w.shape[1]
    tm, tn = _tile(m, tm), _tile(n, tn)
    assert n_rot_cols % tn == 0
    return pl.pallas_call(
        functools.partial(_qkv_kernel, n_rot_tiles=n_rot_cols // tn, tn=tn),
        grid=(m // tm, n // tn),
        in_specs=[pl.BlockSpec((tm, d), lambda i, j: (i, 0)),
                  pl.BlockSpec((1, d), lambda i, j: (0, 0)),
                  pl.BlockSpec((d, tn), lambda i, j: (0, j)),
                  pl.BlockSpec((tm, 1), lambda i, j: (i, 0)),
                  pl.BlockSpec((1, tn), lambda i, j: (0, j)),
                  pl.BlockSpec((1, LANES), lambda i, j: (0, 0)),
                  pl.BlockSpec((LANES, LANES), lambda i, j: (0, 0))],
        out_specs=pl.BlockSpec((tm, tn), lambda i, j: (i, j)),
        out_shape=jax.ShapeDtypeStruct((m, n), BF16),
        scratch_shapes=[pltpu.VMEM((tm, d), BF16), pltpu.VMEM((tm, LANES), F32),
                        pltpu.VMEM((tm, LANES), F32), pltpu.VMEM((tm, LANES), F32)],
        compiler_params=_cparams(("parallel", "arbitrary")),
    )(x, g.reshape(1, d), w, pos, gains, freq, bd128)


def _attn_kernel(fast_ref, q_ref, k_ref, v_ref, lam_ref, subln_ref, o_ref, vt_ref, m_ref, ls_ref, l_ref,
                 acc_ref, *, tq, lambda_init):
    qi = pl.program_id(2)

    @pl.when(qi == 0)
    def _():
        for c in range(vt_ref.shape[0]):
            vt_ref[c] = v_ref[c * tq:(c + 1) * tq, :].astype(F32).T.astype(BF16)

    q = q_ref[...]
    lane = lax.broadcasted_iota(jnp.int32, q.shape, 1)
    zero = jnp.zeros_like(q)
    qs = jnp.concatenate([jnp.where(lane < HEAD, q, zero), jnp.where(lane < HEAD, zero, q)], axis=0)
    acc_ref[...] = jnp.zeros_like(acc_ref)

    def scores(j, masked):
        s = _nt(k_ref[pl.ds(pl.multiple_of(j * tq, tq), tq), :], qs)
        if masked:
            key = lax.broadcasted_iota(jnp.int32, (tq, tq), 0)
            qry = lax.broadcasted_iota(jnp.int32, (tq, tq), 1)
            keep = jnp.concatenate([key <= qry, key <= qry], axis=1)
            s = jnp.where(keep, s, NEG_BIG)
        return s

    def sweep(step):
        def body(j, carry):
            step(j, False)
            return carry
        lax.fori_loop(0, qi, body, 0)
        step(qi, True)

    @pl.when(fast_ref[0] == 1)
    def _():
        l_ref[...] = jnp.zeros_like(l_ref)

        def fold(p):
            return jnp.sum(p.reshape(tq // SUBLANES, SUBLANES, 2 * tq), axis=0)

        def step(j, masked):
            p = jnp.exp2(scores(j, masked))
            l_ref[...] += fold(p)
            acc_ref[...] += _dot(vt_ref[j], p.astype(BF16))

        def pair(jj, carry):
            pa = jnp.exp2(scores(2 * jj, False))
            pb = jnp.exp2(scores(2 * jj + 1, False))
            l_ref[...] += fold(pa) + fold(pb)
            acc_ref[...] += _dot(vt_ref[2 * jj], pa.astype(BF16)) + _dot(vt_ref[2 * jj + 1], pb.astype(BF16))
            return carry

        lax.fori_loop(0, qi // 2, pair, 0)

        @pl.when(qi % 2 == 1)
        def _():
            step(qi - 1, False)

        step(qi, True)
        ls_ref[...] = jnp.sum(l_ref[...], axis=0, keepdims=True)

    @pl.when(fast_ref[0] == 0)
    def _():
        m_ref[...] = jnp.full_like(m_ref, NEG_BIG)
        ls_ref[...] = jnp.zeros_like(ls_ref)

        def step(j, masked):
            s = scores(j, masked)
            m_old = m_ref[...]
            m_new = jnp.maximum(m_old, jnp.max(s, axis=0, keepdims=True))
            alpha = jnp.exp2(m_old - m_new)
            p = jnp.exp2(s - m_new)
            ls_ref[...] = alpha * ls_ref[...] + jnp.sum(p, axis=0, keepdims=True)
            acc_ref[...] = alpha * acc_ref[...] + _dot(vt_ref[j], p.astype(BF16))
            m_ref[...] = m_new

        sweep(step)

    lq = lam_ref[...]
    lam = (jnp.exp(jnp.sum(lq[0:1] * lq[1:2], axis=-1, keepdims=True))
           - jnp.exp(jnp.sum(lq[2:3] * lq[3:4], axis=-1, keepdims=True)) + lambda_init)
    o = acc_ref[...] / ls_ref[...]
    o = o[:, :tq] - lam * o[:, tq:]
    ms = jnp.mean(o * o, axis=0, keepdims=True)
    o = o * lax.rsqrt(ms + RMS_EPS) * subln_ref[...] * (1.0 - lambda_init)
    o_ref[...] = o.T.astype(o_ref.dtype)


def diff_attention(qkv, fast, batch, seq, n_heads, lam_params, subln, lambda_init, *, tq=512):
    m = qkv.shape[0]
    tq = _tile(seq, tq)
    nq = seq // tq
    grid_spec = pltpu.PrefetchScalarGridSpec(
        num_scalar_prefetch=1,
        grid=(batch, n_heads, nq),
        in_specs=[pl.BlockSpec((tq, LANES), lambda b, h, i, f: (b * nq + i, h)),
                  pl.BlockSpec((seq, LANES), lambda b, h, i, f: (b, n_heads + h)),
                  pl.BlockSpec((seq, LANES), lambda b, h, i, f: (b, 2 * n_heads + h)),
                  pl.BlockSpec((4, HEAD), lambda b, h, i, f: (0, 0)),
                  pl.BlockSpec((LANES, 1), lambda b, h, i, f: (0, 0))],
        out_specs=pl.BlockSpec((tq, LANES), lambda b, h, i, f: (b * nq + i, h)),
        scratch_shapes=[pltpu.VMEM((nq, LANES, tq), BF16),
                        pltpu.VMEM((1, 2 * tq), F32), pltpu.VMEM((1, 2 * tq), F32),
                        pltpu.VMEM((SUBLANES, 2 * tq), F32), pltpu.VMEM((LANES, 2 * tq), F32)])
    return pl.pallas_call(
        functools.partial(_attn_kernel, tq=tq, lambda_init=lambda_init),
        grid_spec=grid_spec,
        out_shape=jax.ShapeDtypeStruct((m, n_heads * LANES), BF16),
        compiler_params=_cparams(("parallel", "parallel", "arbitrary")),
    )(fast, qkv, qkv, qkv, lam_params, subln.reshape(LANES, 1))


def _router_kernel(x_ref, g_ref, router_ref, ids_ref, gates_ref):
    h = _rms(x_ref[...], g_ref[...])
    logits = _dot(h, router_ref[...], precision=HIGHEST)
    lane = lax.broadcasted_iota(jnp.int32, logits.shape, 1)
    logits = jnp.where(lane < N_EXPERTS, logits, NEG_BIG)
    v1 = jnp.max(logits, axis=-1, keepdims=True)
    i1 = jnp.min(jnp.where(logits == v1, lane, LANES), axis=-1, keepdims=True)
    rest = jnp.where(lane == i1, NEG_BIG, logits)
    v2 = jnp.max(rest, axis=-1, keepdims=True)
    i2 = jnp.min(jnp.where(rest == v2, lane, LANES), axis=-1, keepdims=True)
    e2 = jnp.exp(v2 - v1)
    ids_ref[...] = jnp.where(lane == 0, i1, i2)
    gates_ref[...] = jnp.where(lane == 0, 1.0 / (1.0 + e2), e2 / (1.0 + e2))


def moe_router(x, g, router_p, *, tm=1024):
    m, d = x.shape
    tm = _tile(m, tm)
    return pl.pallas_call(
        _router_kernel,
        grid=(m // tm,),
        in_specs=[pl.BlockSpec((tm, d), lambda i: (i, 0)),
                  pl.BlockSpec((1, d), lambda i: (0, 0)),
                  pl.BlockSpec((d, LANES), lambda i: (0, 0))],
        out_specs=[pl.BlockSpec((tm, LANES), lambda i: (i, 0))] * 2,
        out_shape=[jax.ShapeDtypeStruct((m, LANES), jnp.int32), jax.ShapeDtypeStruct((m, LANES), F32)],
        compiler_params=_cparams(("parallel",)),
    )(x, g.reshape(1, d), router_p)


def _route_tables(ids, tm, n_tiles):
    n_pairs = ids.shape[0] * 2
    n_rows = n_tiles * tm
    shift = max(n_pairs, n_rows).bit_length()
    e_flat = ids.reshape(-1)
    experts = jnp.arange(N_EXPERTS, dtype=jnp.int32)
    counts = jnp.sum((e_flat[:, None] == experts[None, :]).astype(jnp.int32), axis=0)
    padded = ((counts + tm - 1) // tm) * tm
    ends = jnp.cumsum(padded)
    pad_ends = jnp.cumsum(padded - counts)
    q = jnp.arange(n_rows - n_pairs, dtype=jnp.int32)
    pad_expert = jnp.sum((q[:, None] >= pad_ends[None, :]).astype(jnp.int32), axis=1)
    low = (1 << shift) - 1
    keys = jnp.concatenate([(e_flat << shift) | jnp.arange(n_pairs, dtype=jnp.int32),
                            (pad_expert << shift) | low])
    keys = jnp.sort(keys)
    perm = jnp.where((keys & low) == low, -1, keys & low)
    starts = jnp.arange(n_tiles, dtype=jnp.int32) * tm
    tile_expert = jnp.sum((starts[:, None] >= ends[None, :]).astype(jnp.int32), axis=1)
    n_valid = ends[-1] // tm
    last_expert = jnp.sum(jnp.where(jnp.arange(n_tiles) == n_valid - 1, tile_expert, 0))
    tile_expert = jnp.where(jnp.arange(n_tiles) < n_valid, tile_expert, last_expert)
    n_real = jnp.sum((perm >= 0).reshape(n_tiles, tm).astype(jnp.int32), axis=1)
    return perm, tile_expert.astype(jnp.int32), n_valid.reshape(1).astype(jnp.int32), n_real


def _moe_group_kernel(te_ref, nv_ref, perm_ref, nr_ref, x_hbm, g_ref, wg_ref, wu_ref, wd_ref, y_hbm,
                      xbuf, ybuf, h_ref, acc_ref, gsem, ssem, *, tm, n_tokens):
    i = pl.program_id(0)
    f = pl.program_id(1)
    valid = i < nv_ref[0]
    base = i * tm
    d = h_ref.shape[1]
    n_chunks = d // LANES

    def tile_rows(k):
        return pl.ds(pl.multiple_of(k * SUBLANES, SUBLANES), SUBLANES)

    slot = i % 2

    def tile_in(s, r, t):
        return pltpu.make_async_copy(x_hbm.at[tile_rows(t), :], xbuf.at[s, tile_rows(r), :], gsem.at[s])

    def tile_out(r, dst):
        return pltpu.make_async_copy(ybuf.at[tile_rows(r), :], y_hbm.at[tile_rows(dst), :], ssem)

    def chunk_rows(c):
        return pl.ds(c, tm, stride=SUBLANES)

    def start_gather(tile, s):
        def start(r, carry):
            j = perm_ref[tile * tm + r]
            tile_in(s, r, jnp.maximum(j, 0) >> 1).start()
            return carry
        lax.fori_loop(0, tm, start, 0, unroll=8)

    def wait_scatter(count):
        def wait(r, carry):
            tile_out(r, 0).wait()
            return carry
        lax.fori_loop(0, count, wait, 0)

    @pl.when(valid & (f == 0))
    def _():
        @pl.when(i == 0)
        def _():
            start_gather(0, 0)

        def wait(r, carry):
            tile_in(slot, r, 0).wait()
            return carry
        lax.fori_loop(0, tm, wait, 0, unroll=8)

        @pl.when(i + 1 < nv_ref[0])
        def _():
            start_gather(i + 1, 1 - slot)

        ss = jnp.zeros((tm, 1), F32)
        for c in range(n_chunks):
            xc = xbuf[slot, chunk_rows(c), :]
            ss = ss + jnp.sum(xc * xc, axis=-1, keepdims=True)
        inv = lax.rsqrt(ss * (1.0 / d) + RMS_EPS)
        for c in range(n_chunks):
            sl = slice(c * LANES, (c + 1) * LANES)
            h_ref[:, sl] = (xbuf[slot, chunk_rows(c), :] * inv * g_ref[:, sl]).astype(BF16)
        acc_ref[...] = jnp.zeros_like(acc_ref)

    @pl.when(valid)
    def _():
        h = h_ref[...]
        gate = _dot(h, wg_ref[...])
        up = _dot(h, wu_ref[...])
        act = (gate * _sigmoid(gate) * up).astype(BF16)
        acc_ref[...] += _dot(act, wd_ref[...])

    @pl.when(valid & (f == pl.num_programs(1) - 1))
    def _():
        @pl.when(i > 0)
        def _():
            wait_scatter(nr_ref[jnp.maximum(i - 1, 0)])

        for c in range(n_chunks):
            ybuf[chunk_rows(c), :] = acc_ref[:, c * LANES:(c + 1) * LANES]

        n_real = nr_ref[i]

        def start(r):
            j = perm_ref[base + r]
            tile_out(r, (j & 1) * n_tokens + (j >> 1)).start()

        def start8(r8, carry):
            for k in range(SUBLANES):
                start(r8 * SUBLANES + k)
            return carry
        lax.fori_loop(0, n_real // SUBLANES, start8, 0)

        def start1(r, carry):
            start(r)
            return carry
        lax.fori_loop((n_real // SUBLANES) * SUBLANES, n_real, start1, 0)

        @pl.when(i == nv_ref[0] - 1)
        def _():
            wait_scatter(n_real)


def moe_experts(x8, g, perm, tile_expert, n_valid, n_real, wg, wu, wd, *, tm, tf=896):
    d = g.shape[0]
    m = x8.shape[0] // SUBLANES
    f = wg.shape[2]
    tf = _tile(f, tf)
    n_tiles, n_f = tile_expert.shape[0], f // tf
    fidx = lambda i, j, nv: jnp.where(i < nv[0], j, n_f - 1)
    grid_spec = pltpu.PrefetchScalarGridSpec(
        num_scalar_prefetch=4,
        grid=(n_tiles, n_f),
        in_specs=[pl.BlockSpec(memory_space=pl.ANY),
                  pl.BlockSpec((1, d), lambda i, j, te, nv, pm, nr: (0, 0)),
                  pl.BlockSpec((None, d, tf), lambda i, j, te, nv, pm, nr: (te[i], 0, fidx(i, j, nv))),
                  pl.BlockSpec((None, d, tf), lambda i, j, te, nv, pm, nr: (te[i], 0, fidx(i, j, nv))),
                  pl.BlockSpec((None, tf, d), lambda i, j, te, nv, pm, nr: (te[i], fidx(i, j, nv), 0))],
        out_specs=pl.BlockSpec(memory_space=pl.ANY),
        scratch_shapes=[pltpu.VMEM((2, tm * SUBLANES, LANES), F32), pltpu.VMEM((tm * SUBLANES, LANES), F32),
                        pltpu.VMEM((tm, d), BF16), pltpu.VMEM((tm, d), F32),
                        pltpu.SemaphoreType.DMA((2,)), pltpu.SemaphoreType.DMA(())])
    return pl.pallas_call(
        functools.partial(_moe_group_kernel, tm=tm, n_tokens=m),
        grid_spec=grid_spec,
        out_shape=jax.ShapeDtypeStruct((2 * m * SUBLANES, LANES), F32),
        compiler_params=_cparams(("arbitrary", "arbitrary")),
    )(tile_expert, n_valid, perm, n_real, x8, g.reshape(1, d), wg, wu, wd)


def _moe_combine_kernel(x_ref, y0_ref, y1_ref, gates_ref, o_ref, *, tm):
    gates = gates_ref[...]
    g0, g1 = gates[:, 0:1], gates[:, 1:2]
    for c in range(x_ref.shape[1] // LANES):
        sl = slice(c * LANES, (c + 1) * LANES)
        rows = pl.ds(c, tm, stride=SUBLANES)
        o_ref[:, sl] = x_ref[:, sl] + g0 * y0_ref[rows, :] + g1 * y1_ref[rows, :]


def moe_combine(x, y8, gates, *, tm=1024):
    m, d = x.shape
    tm = _tile(m, tm)
    nb = m // tm
    return pl.pallas_call(
        functools.partial(_moe_combine_kernel, tm=tm),
        grid=(nb,),
        in_specs=[pl.BlockSpec((tm, d), lambda i: (i, 0)),
                  pl.BlockSpec((tm * SUBLANES, LANES), lambda i: (i, 0)),
                  pl.BlockSpec((tm * SUBLANES, LANES), lambda i: (nb + i, 0)),
                  pl.BlockSpec((tm, LANES), lambda i: (i, 0))],
        out_specs=pl.BlockSpec((tm, d), lambda i: (i, 0)),
        out_shape=jax.ShapeDtypeStruct((m, d), F32),
        compiler_params=_cparams(("parallel",)),
    )(x, y8, y8, gates)


def moe(x, g, router_p, wg, wu, wd, *, tm=1024):
    m, d = x.shape
    assert d == SUBLANES * LANES
    tm = _tile(m, tm)
    n_tiles = (2 * m) // tm + N_EXPERTS
    ids, gates = moe_router(x, g, router_p)
    perm, tile_expert, n_valid, n_real = _route_tables(ids[:, :2], tm, n_tiles)
    x8 = x.reshape(m * SUBLANES, LANES)
    y8 = moe_experts(x8, g, perm, tile_expert, n_valid, n_real, wg, wu, wd, tm=tm)
    return moe_combine(x, y8, gates, tm=tm)


def _block_diag(blocks):
    n, h, _ = blocks.shape
    eye = jnp.eye(n, dtype=blocks.dtype)
    return (eye[:, None, :, None] * blocks[:, :, None, :]).reshape(n * h, n * h)


def _seg_ones(n):
    seg = jnp.arange(n) // HEAD
    return (seg[:, None] == seg[None, :]).astype(BF16)


def _even_layer(x, batch, seq, ln_mix, w_in, conv_w, conv_b, gate_a_w, gate_a_b, gate_x_w, gate_x_b,
                lru_lambda, shift_mu, w0, w2, a0, a2, g2, k_k, k_a, r_k, gn_w, gn_b, w_out,
                ln_ffn, ffn_gate, ffn_up, ffn_down):
    lru_w = lru_lambda.shape[0]
    rw_w = w0.shape[0]
    dl, al, gl = w2.shape[0], a2.shape[0], g2.shape[0]
    n_in = w_in.shape[1]
    proj = norm_matmul(x, ln_mix, w_in.astype(BF16), tn=n_in // 2)

    y_lru = lru_branch(proj, batch, seq, conv_w, conv_b,
                       _block_diag(gate_a_w).astype(BF16), gate_a_b,
                       _block_diag(gate_x_w).astype(BF16), gate_x_b, lru_lambda)

    lora = dl + al + gl
    zeros = lambda n: jnp.zeros((n, rw_w), F32)
    w2p = jnp.concatenate([w2, zeros(al + gl)], axis=0).astype(BF16)
    a2p = jnp.concatenate([zeros(dl), a2, zeros(gl)], axis=0).astype(BF16)
    g2p = jnp.concatenate([zeros(dl + al), g2], axis=0).astype(BF16)
    assert lora == w2p.shape[0]
    r, k, v, a, b, lw, g = rwkv_prep(proj, seq, 2 * lru_w, shift_mu, w0, w2p, a0, a2p, g2p, k_k, k_a,
                                     _seg_ones(rw_w))
    y_rwkv = rwkv_scan(r, k, v, a, b, lw, g, batch, seq, r_k.reshape(-1), gn_w, gn_b, _seg_ones(LANES))

    w_out = w_out.astype(BF16)
    x = matmul_residual([y_lru, y_rwkv], [w_out[:lru_w], w_out[lru_w:]], x)
    return ffn(x, ln_ffn, ffn_gate.astype(BF16), ffn_up.astype(BF16), ffn_down.astype(BF16))


def _odd_layer(x, pos, batch, seq, layer_idx, ln_mix, w_qkv, q_norm, k_norm, lq1, lk1, lq2, lk2, subln,
               w_o, ln_ffn, router, moe_gate, moe_up, moe_down):
    d = x.shape[1]
    n_heads = d // (2 * HEAD)
    qd = n_heads * 2 * HEAD
    lambda_init = 0.8 - 0.6 * math.exp(-0.3 * layer_idx)
    reps = qd // HEAD
    q_gain = q_norm * (HEAD ** -0.5 * LOG2E)
    logit_bound = 1.02 * HEAD * jnp.max(jnp.abs(q_gain)) * jnp.max(jnp.abs(k_norm))
    fast = (logit_bound <= MAX_EXP2_ARG).astype(jnp.int32).reshape(1)
    gains = jnp.concatenate([jnp.tile(q_gain, reps),
                             jnp.tile(k_norm, reps),
                             jnp.ones((w_qkv.shape[1] - 2 * qd,), F32)])[None, :]
    seg = jnp.arange(LANES) % HEAD
    inv_freq = ROPE_THETA ** (-(2.0 * (seg % (ROPE_DIM // 2))).astype(F32) / ROPE_DIM)
    freq = jnp.where(seg < ROPE_DIM, inv_freq, 0.0)[None, :].astype(F32)
    qkv = qkv_project(x, ln_mix, w_qkv.astype(BF16), pos, gains, freq, _seg_ones(LANES), 2 * qd)
    lam_params = jnp.stack([lq1, lk1, lq2, lk2]).astype(F32)
    o = diff_attention(qkv, fast, batch, seq, n_heads, lam_params, subln, lambda_init)
    x = matmul_residual([o], [w_o.astype(BF16)], x)
    router_p = jnp.pad(router, ((0, 0), (0, LANES - router.shape[1])))
    return moe(x, ln_ffn, router_p, moe_gate.astype(BF16), moe_up.astype(BF16), moe_down.astype(BF16))


def kernel(x, positions, e_ln_mix, e_w_in, e_conv_w, e_conv_b, e_gate_a_w, e_gate_a_b, e_gate_x_w, e_gate_x_b, e_lru_lambda, e_shift_mu, e_w0, e_w2, e_a0, e_a2, e_g2, e_k_k, e_k_a, e_r_k, e_gn_w, e_gn_b, e_w_out, e_ln_ffn, e_ffn_gate, e_ffn_up, e_ffn_down, o_ln_mix, o_w_qkv, o_q_norm, o_k_norm, o_lambda_q1, o_lambda_k1, o_lambda_q2, o_lambda_k2, o_subln, o_w_o, o_ln_ffn, o_router, o_moe_gate, o_moe_up, o_moe_down):
    batch, seq, d = x.shape
    depth = e_ln_mix.shape[0] + o_ln_mix.shape[0]
    xf = x.reshape(batch * seq, d)
    pos = positions.reshape(batch * seq, 1).astype(F32)
    for i in range(depth):
        j = i // 2
        if i % 2 == 0:
            xf = _even_layer(xf, batch, seq, e_ln_mix[j], e_w_in[j], e_conv_w[j], e_conv_b[j],
                             e_gate_a_w[j], e_gate_a_b[j], e_gate_x_w[j], e_gate_x_b[j],
                             e_lru_lambda[j], e_shift_mu[j], e_w0[j], e_w2[j], e_a0[j], e_a2[j],
                             e_g2[j], e_k_k[j], e_k_a[j], e_r_k[j], e_gn_w[j], e_gn_b[j], e_w_out[j],
                             e_ln_ffn[j], e_ffn_gate[j], e_ffn_up[j], e_ffn_down[j])
        else:
            xf = _odd_layer(xf, pos, batch, seq, i, o_ln_mix[j], o_w_qkv[j], o_q_norm[j], o_k_norm[j],
                            o_lambda_q1[j], o_lambda_k1[j], o_lambda_q2[j], o_lambda_k2[j], o_subln[j],
                            o_w_o[j], o_ln_ffn[j], o_router[j], o_moe_gate[j], o_moe_up[j],
                            o_moe_down[j])
    return xf.reshape(batch, seq, d)
```

```python
import functools
import math

import jax
import jax.numpy as jnp
from jax import lax
from jax.experimental import pallas as pl
from jax.experimental.pallas import tpu as pltpu

F32 = jnp.float32
BF16 = jnp.bfloat16
HIGHEST = lax.Precision.HIGHEST

LANES = 128
SUBLANES = 8
VMEM_LIMIT = 56 * 1024 * 1024

HEAD = 64
CONV_WIDTH = 4
LRU_C = 8.0
GN_EPS = 64e-5
RMS_EPS = 1e-6
ROPE_DIM = 16
ROPE_THETA = 500000.0
N_EXPERTS = 8
CHUNK = 64
NEG_BIG = -1e30
KV_GROUP = 4
LOG2E = 1.4426950408889634
MAX_EXP2_ARG = 60.0


def _cparams(sem):
    return pltpu.CompilerParams(dimension_semantics=sem, vmem_limit_bytes=VMEM_LIMIT)


def _tile(n, pref):
    t = min(n, pref)
    assert n % t == 0, (n, pref)
    return t


def _nt(a, b, **kw):
    return lax.dot_general(a, b, (((1,), (1,)), ((), ())), preferred_element_type=F32, **kw)


def _tn(a, b, **kw):
    return lax.dot_general(a, b, (((0,), (0,)), ((), ())), preferred_element_type=F32, **kw)


def _dot(a, b, **kw):
    return jnp.dot(a, b, preferred_element_type=F32, **kw)


def _segsum(x, bd):
    hi = x.astype(BF16)
    lo = (x - hi.astype(F32)).astype(BF16)
    return _dot(hi, bd) + _dot(lo, bd)


def _rms(x, g):
    ms = jnp.mean(x * x, axis=-1, keepdims=True)
    return x * lax.rsqrt(ms + RMS_EPS) * g


def _sigmoid(x):
    return 1.0 / (1.0 + jnp.exp(-x))


def _softplus(x):
    return jnp.maximum(x, 0.0) + jnp.log1p(jnp.exp(-jnp.abs(x)))


def _norm_mm_kernel(x_ref, g_ref, w_ref, o_ref, h_ref):
    @pl.when(pl.program_id(1) == 0)
    def _():
        h_ref[...] = _rms(x_ref[...], g_ref[...]).astype(BF16)

    o_ref[...] = _dot(h_ref[...], w_ref[...]).astype(o_ref.dtype)


def norm_matmul(x, g, w, *, tm=1024, tn=512, out_dtype=F32):
    m, d = x.shape
    n = w.shape[1]
    tm, tn = _tile(m, tm), _tile(n, tn)
    return pl.pallas_call(
        _norm_mm_kernel,
        grid=(m // tm, n // tn),
        in_specs=[pl.BlockSpec((tm, d), lambda i, j: (i, 0)),
                  pl.BlockSpec((1, d), lambda i, j: (0, 0)),
                  pl.BlockSpec((d, tn), lambda i, j: (0, j))],
        out_specs=pl.BlockSpec((tm, tn), lambda i, j: (i, j)),
        out_shape=jax.ShapeDtypeStruct((m, n), out_dtype),
        scratch_shapes=[pltpu.VMEM((tm, d), BF16)],
        compiler_params=_cparams(("parallel", "arbitrary")),
    )(x, g.reshape(1, d), w)


def _mm_res_kernel(*refs, n_in):
    ys, ws = refs[:n_in], refs[n_in:2 * n_in]
    res_ref, o_ref = refs[2 * n_in], refs[2 * n_in + 1]
    acc = res_ref[...]
    for y_ref, w_ref in zip(ys, ws):
        acc = acc + _dot(y_ref[...], w_ref[...])
    o_ref[...] = acc


def matmul_residual(ys, ws, res, *, tm=1024, tn=512):
    m, n = res.shape
    tm, tn = _tile(m, tm), _tile(n, tn)
    n_in = len(ys)
    in_specs = [pl.BlockSpec((tm, y.shape[1]), lambda i, j: (i, 0)) for y in ys]
    in_specs += [pl.BlockSpec((w.shape[0], tn), lambda i, j: (0, j)) for w in ws]
    in_specs += [pl.BlockSpec((tm, tn), lambda i, j: (i, j))]
    return pl.pallas_call(
        functools.partial(_mm_res_kernel, n_in=n_in),
        grid=(m // tm, n // tn),
        in_specs=in_specs,
        out_specs=pl.BlockSpec((tm, tn), lambda i, j: (i, j)),
        out_shape=jax.ShapeDtypeStruct((m, n), F32),
        compiler_params=_cparams(("parallel", "arbitrary")),
    )(*ys, *ws, res)


def _lru_kernel(x_ref, gate_ref, cw_ref, cb_ref, wa_ref, ba_ref, wx_ref, bx_ref, lam_ref,
                o_ref, tail_ref, h_ref, *, tm):
    @pl.when(pl.program_id(1) == 0)
    def _():
        tail_ref[...] = jnp.zeros_like(tail_ref)
        h_ref[...] = jnp.zeros_like(h_ref)

    x = x_ref[...]
    xx = jnp.concatenate([tail_ref[...], x], axis=0)
    tail_ref[...] = x[tm - 8:, :]
    cw = cw_ref[...]
    xc = cb_ref[...] + cw[CONV_WIDTH - 1:CONV_WIDTH, :] * x
    for s in range(1, CONV_WIDTH):
        xc = xc + cw[CONV_WIDTH - 1 - s:CONV_WIDTH - s, :] * pltpu.roll(xx, s, axis=0)[8:, :]

    xb = xc.astype(BF16)
    gate_a = _dot(xb, wa_ref[...]) + ba_ref[...]
    gate_x = _dot(xb, wx_ref[...]) + bx_ref[...]
    log_a = -LRU_C * _sigmoid(gate_a) * _softplus(-lam_ref[...])
    a = jnp.exp(log_a)
    u = jnp.sqrt(1.0 - a * a) * _sigmoid(gate_x) * xc

    row = lax.broadcasted_iota(jnp.int32, a.shape, 0)
    s = 1
    while s < tm:
        keep = row >= s
        a_s = jnp.where(keep, pltpu.roll(a, s, axis=0), 1.0)
        u_s = jnp.where(keep, pltpu.roll(u, s, axis=0), 0.0)
        u = a * u_s + u
        a = a * a_s
        s *= 2
    h = a * h_ref[...] + u
    h_ref[...] = h[tm - 1:tm, :]
    o_ref[...] = (jax.nn.gelu(gate_ref[...]) * h).astype(o_ref.dtype)


def lru_branch(proj, batch, seq, conv_w, conv_b, wa_bd, ba, wx_bd, bx, lam, *, tm=256):
    m = proj.shape[0]
    w = lam.shape[0]
    tm = _tile(seq, tm)
    nt = seq // tm
    vec = lambda: pl.BlockSpec((1, w), lambda b, i: (0, 0))
    return pl.pallas_call(
        functools.partial(_lru_kernel, tm=tm),
        grid=(batch, nt),
        in_specs=[pl.BlockSpec((tm, w), lambda b, i: (b * nt + i, 0)),
                  pl.BlockSpec((tm, w), lambda b, i: (b * nt + i, 1)),
                  pl.BlockSpec((CONV_WIDTH, w), lambda b, i: (0, 0)),
                  vec(),
                  pl.BlockSpec((w, w), lambda b, i: (0, 0)), vec(),
                  pl.BlockSpec((w, w), lambda b, i: (0, 0)), vec(),
                  vec()],
        out_specs=pl.BlockSpec((tm, w), lambda b, i: (b * nt + i, 0)),
        out_shape=jax.ShapeDtypeStruct((m, w), BF16),
        scratch_shapes=[pltpu.VMEM((8, w), F32), pltpu.VMEM((1, w), F32)],
        compiler_params=_cparams(("parallel", "arbitrary")),
    )(proj, proj, conv_w, conv_b.reshape(1, w), wa_bd, ba.reshape(1, w), wx_bd, bx.reshape(1, w),
      lam.reshape(1, w))


def _rwkv_prep_kernel(pr_ref, pk_ref, pv_ref, pl_ref, qr_ref, qk_ref, qv_ref, ql_ref,
                      mur_ref, muk_ref, muv_ref, mul_ref, w0_ref, w2_ref, a0_ref, a2_ref, g2_ref,
                      kk_ref, ka_ref, bd_ref,
                      r_out, k_out, v_out, a_out, b_out, lw_out, g_out, *, tm, seq):
    first = (pl.program_id(0) * tm) % seq == 0

    def shift_mix(p_ref, q_ref, mu_ref):
        x = p_ref[...]
        prev = jnp.where(first, 0.0, q_ref[7:8, :])
        row = lax.broadcasted_iota(jnp.int32, x.shape, 0)
        xs = jnp.where(row == 0, prev, pltpu.roll(x, 1, axis=0))
        return x + (xs - x) * mu_ref[...]

    r = shift_mix(pr_ref, qr_ref, mur_ref)
    k = shift_mix(pk_ref, qk_ref, muk_ref)
    v = shift_mix(pv_ref, qv_ref, muv_ref)
    lo = shift_mix(pl_ref, ql_ref, mul_ref)

    wlog = -_softplus(-(w0_ref[...] + _dot(jnp.tanh(lo).astype(BF16), w2_ref[...]))) - 0.5
    a = _sigmoid(a0_ref[...] + _dot(lo.astype(BF16), a2_ref[...]))
    g = _dot(_sigmoid(lo).astype(BF16), g2_ref[...])

    kk = k * kk_ref[...]
    nrm = jnp.sqrt(_segsum(kk * kk, bd_ref[...]))
    kk = kk / jnp.maximum(nrm, 1e-12)

    r_out[...] = r
    k_out[...] = k * (1.0 + (a - 1.0) * ka_ref[...])
    v_out[...] = v
    a_out[...] = -kk
    b_out[...] = kk * a
    lw_out[...] = -jnp.exp(wlog)
    g_out[...] = g


def rwkv_prep(proj, seq, col0, mu, w0, w2p, a0, a2p, g2p, k_k, k_a, bd, *, tm=512):
    m = proj.shape[0]
    w = w0.shape[0]
    lw = w2p.shape[0]
    tm = _tile(seq, tm)
    cb = col0 // w
    lb = (col0 + 3 * w) // lw
    main = lambda c, width: pl.BlockSpec((tm, width), lambda i: (i, c))
    prev = lambda c, width: pl.BlockSpec((8, width), lambda i: (jnp.maximum(i * (tm // 8) - 1, 0), c))
    vec = lambda width: pl.BlockSpec((1, width), lambda i: (0, 0))
    mat = lambda a: pl.BlockSpec(a.shape, lambda i: (0, 0))
    mu_r, mu_k, mu_v, mu_l = (mu[None, 0:w], mu[None, w:2 * w], mu[None, 2 * w:3 * w], mu[None, 3 * w:])
    outs = [jax.ShapeDtypeStruct((m, w), F32)] * 7
    return pl.pallas_call(
        functools.partial(_rwkv_prep_kernel, tm=tm, seq=seq),
        grid=(m // tm,),
        in_specs=[main(cb, w), main(cb + 1, w), main(cb + 2, w), main(lb, lw),
                  prev(cb, w), prev(cb + 1, w), prev(cb + 2, w), prev(lb, lw),
                  vec(w), vec(w), vec(w), vec(lw),
                  vec(w), mat(w2p), vec(w), mat(a2p), mat(g2p), vec(w), vec(w), mat(bd)],
        out_specs=[pl.BlockSpec((tm, w), lambda i: (i, 0))] * 7,
        out_shape=outs,
        compiler_params=_cparams(("parallel",)),
    )(proj, proj, proj, proj, proj, proj, proj, proj, mu_r, mu_k, mu_v, mu_l,
      w0.reshape(1, w), w2p, a0.reshape(1, w), a2p, g2p, k_k.reshape(1, w), k_a.reshape(1, w), bd)


def _rwkv_scan_kernel(r_ref, k_ref, v_ref, a_ref, b_ref, lw_ref, g_ref, rk_ref, gnw_ref, gnb_ref,
                      bd_ref, o_ref, s_ref, *, batch, width):
    @pl.when(pl.program_id(0) == 0)
    def _():
        s_ref[...] = jnp.zeros_like(s_ref)

    c = CHUNK
    tri = (lax.broadcasted_iota(jnp.int32, (c, c), 0) >= lax.broadcasted_iota(jnp.int32, (c, c), 1))
    lane = lax.broadcasted_iota(jnp.int32, (c, LANES), 1)
    head0 = lane < HEAD
    i2 = lax.broadcasted_iota(jnp.int32, (2 * c, 2 * c), 0)
    j2 = lax.broadcasted_iota(jnp.int32, (2 * c, 2 * c), 1)
    strict = i2 > j2
    incl = i2 >= j2
    bd = bd_ref[...]
    n_steps = int(math.log2(c))

    def stack(x):
        xb = x.astype(BF16)
        zero = jnp.zeros_like(xb)
        return jnp.concatenate([jnp.where(head0, xb, zero), jnp.where(head0, zero, xb)], axis=0)

    chains = []
    for bi in range(batch):
        lw = lw_ref[bi]
        cum = _dot(tri.astype(F32), lw, precision=HIGHEST)
        tot = cum[c - 1:c, :]
        g_out = jnp.exp(-cum)
        g_suf = jnp.exp(tot - cum)
        g_tot = jnp.exp(tot)
        r, k, v, b = r_ref[bi], k_ref[bi], v_ref[bi], b_ref[bi]
        rt = r * jnp.exp(cum)
        kt = k * g_out
        at = a_ref[bi] * jnp.exp(cum - lw)
        bt = b * g_out
        ks = k * g_suf
        bs = b * g_suf
        rkv = r * k * rk_ref[...]
        for p in range(width // LANES):
            sl = slice(p * LANES, (p + 1) * LANES)
            chains.append(dict(
                bi=bi, p=p, sl=sl, v=v[:, sl], rkv=rkv[:, sl], g_tot=g_tot[:, sl],
                v_s=stack(v[:, sl]),
                ar=jnp.concatenate([stack(at[:, sl]), stack(rt[:, sl])], axis=0),
                kb=jnp.concatenate([stack(kt[:, sl]), stack(bt[:, sl])], axis=0),
                suf=jnp.concatenate([stack(ks[:, sl]), stack(bs[:, sl])], axis=0)))

    for ch in chains:
        ch["state"] = s_ref[ch["bi"], ch["p"]]
        ch["gram"] = _nt(ch["ar"], ch["kb"])
        ch["xs"] = _nt(ch["ar"], ch["state"].astype(BF16))
    for ch in chains:
        gram = ch["gram"]
        ak = jnp.where(strict, gram[:2 * c, :2 * c], 0.0)
        rk = jnp.where(incl, gram[2 * c:, :2 * c], 0.0)
        ch["lpow"] = jnp.where(strict, gram[:2 * c, 2 * c:], 0.0)
        ch["rb"] = jnp.where(incl, gram[2 * c:, 2 * c:], 0.0).astype(BF16)
        ch["lv"] = _dot(jnp.concatenate([ak, rk], axis=0).astype(BF16), ch["v_s"])
    for ch in chains:
        ch["u"] = ch["xs"][:2 * c] + ch["lv"][:2 * c]
    for i in range(n_steps):
        for ch in chains:
            lb = ch["lpow"].astype(BF16)
            ch["u"] = ch["u"] + _dot(lb, ch["u"].astype(BF16))
            if i + 1 < n_steps:
                ch["lpow"] = _dot(lb, lb)
    for ch in chains:
        ub = ch["u"].astype(BF16)
        o_s = ch["xs"][2 * c:] + ch["lv"][2 * c:] + _dot(ch["rb"], ub)
        ch["o"] = o_s[:c] + o_s[c:]
        upd = _tn(jnp.concatenate([ch["v_s"], ub], axis=0), ch["suf"])
        s_ref[ch["bi"], ch["p"]] = ch["state"] * ch["g_tot"] + upd
    for ch in chains:
        bi, sl, o = ch["bi"], ch["sl"], ch["o"]
        mean = _segsum(o, bd) * (1.0 / HEAD)
        d = o - mean
        var = _segsum(d * d, bd) * (1.0 / HEAD)
        on = d * lax.rsqrt(var + GN_EPS) * gnw_ref[:, sl] + gnb_ref[:, sl]
        bonus = _segsum(ch["rkv"], bd) * ch["v"]
        o_ref[bi, :, sl] = ((on + bonus) * g_ref[bi, :, sl]).astype(o_ref.dtype)


def rwkv_scan(r, k, v, a, b, lw, g, batch, seq, r_k, gn_w, gn_b, bd128):
    m, w = r.shape
    nc = seq // CHUNK
    blk = lambda: pl.BlockSpec((batch, CHUNK, w), lambda ci: (0, ci, 0))
    vec = lambda: pl.BlockSpec((1, w), lambda ci: (0, 0))
    as3d = lambda x: x.reshape(batch, seq, w)
    out = pl.pallas_call(
        functools.partial(_rwkv_scan_kernel, batch=batch, width=w),
        grid=(nc,),
        in_specs=[blk() for _ in range(7)] + [vec(), vec(), vec(),
                                               pl.BlockSpec((LANES, LANES), lambda ci: (0, 0))],
        out_specs=blk(),
        out_shape=jax.ShapeDtypeStruct((batch, seq, w), BF16),
        scratch_shapes=[pltpu.VMEM((batch, w // LANES, LANES, LANES), F32)],
        compiler_params=_cparams(("arbitrary",)),
    )(as3d(r), as3d(k), as3d(v), as3d(a), as3d(b), as3d(lw), as3d(g),
      r_k.reshape(1, w), gn_w.reshape(1, w), gn_b.reshape(1, w), bd128)
    return out.reshape(m, w)


def _ffn_kernel(x_ref, g_ref, wg_ref, wu_ref, wd_ref, o_ref, h_ref, acc_ref):
    f = pl.program_id(1)

    @pl.when(f == 0)
    def _():
        h_ref[...] = _rms(x_ref[...], g_ref[...]).astype(BF16)
        acc_ref[...] = x_ref[...]

    h = h_ref[...]
    gate = _dot(h, wg_ref[...])
    up = _dot(h, wu_ref[...])
    act = (gate * _sigmoid(gate) * up).astype(BF16)
    acc_ref[...] += _dot(act, wd_ref[...])

    @pl.when(f == pl.num_programs(1) - 1)
    def _():
        o_ref[...] = acc_ref[...]


def ffn(x, g, wg, wu, wd, *, tm=1024, tf=256):
    m, d = x.shape
    f = wg.shape[1]
    tm, tf = _tile(m, tm), _tile(f, tf)
    return pl.pallas_call(
        _ffn_kernel,
        grid=(m // tm, f // tf),
        in_specs=[pl.BlockSpec((tm, d), lambda i, j: (i, 0)),
                  pl.BlockSpec((1, d), lambda i, j: (0, 0)),
                  pl.BlockSpec((d, tf), lambda i, j: (0, j)),
                  pl.BlockSpec((d, tf), lambda i, j: (0, j)),
                  pl.BlockSpec((tf, d), lambda i, j: (j, 0))],
        out_specs=pl.BlockSpec((tm, d), lambda i, j: (i, 0)),
        out_shape=jax.ShapeDtypeStruct((m, d), F32),
        scratch_shapes=[pltpu.VMEM((tm, d), BF16), pltpu.VMEM((tm, d), F32)],
        compiler_params=_cparams(("parallel", "arbitrary")),
    )(x, g.reshape(1, d), wg, wu, wd)


def _qkv_kernel(x_ref, g_ref, w_ref, pos_ref, gain_ref, freq_ref, bd_ref, o_ref,
                h_ref, cos_ref, sina_ref, sinb_ref, *, n_rot_tiles, tn):
    j = pl.program_id(1)

    @pl.when(j == 0)
    def _():
        h_ref[...] = _rms(x_ref[...], g_ref[...]).astype(BF16)
        ang = pos_ref[...] * freq_ref[...]
        seg = lax.broadcasted_iota(jnp.int32, ang.shape, 1) % HEAD
        half = ROPE_DIM // 2
        cos_ref[...] = jnp.where(seg < ROPE_DIM, jnp.cos(ang), 1.0)
        sin = jnp.sin(ang)
        sina_ref[...] = jnp.where(seg < half, -sin, 0.0)
        sinb_ref[...] = jnp.where((seg >= half) & (seg < ROPE_DIM), sin, 0.0)

    y = _dot(h_ref[...], w_ref[...])

    @pl.when(j < n_rot_tiles)
    def _():
        half = ROPE_DIM // 2
        for c in range(tn // LANES):
            sl = slice(c * LANES, (c + 1) * LANES)
            yc = y[:, sl]
            ms = _segsum(yc * yc, bd_ref[...]) * (1.0 / HEAD)
            yn = yc * lax.rsqrt(ms + RMS_EPS) * gain_ref[:, sl]
            rot = (yn * cos_ref[...] + pltpu.roll(yn, LANES - half, axis=1) * sina_ref[...]
                   + pltpu.roll(yn, half, axis=1) * sinb_ref[...])
            o_ref[:, sl] = rot.astype(o_ref.dtype)

    @pl.when(j >= n_rot_tiles)
    def _():
        o_ref[...] = y.astype(o_ref.dtype)


def qkv_project(x, g, w, pos, gains, freq, bd128, n_rot_cols, *, tm=1024, tn=512):
    m, d = x.shape
    n = w.shape[1]
    tm, tn = _tile(m, tm), _tile(n, tn)
    assert n_rot_cols % tn == 0
    return pl.pallas_call(
        functools.partial(_qkv_kernel, n_rot_tiles=n_rot_cols // tn, tn=tn),
        grid=(m // tm, n // tn),
        in_specs=[pl.BlockSpec((tm, d), lambda i, j: (i, 0)),
                  pl.BlockSpec((1, d), lambda i, j: (0, 0)),
                  pl.BlockSpec((d, tn), lambda i, j: (0, j)),
                  pl.BlockSpec((tm, 1), lambda i, j: (i, 0)),
                  pl.BlockSpec((1, tn), lambda i, j: (0, j)),
                  pl.BlockSpec((1, LANES), lambda i, j: (0, 0)),
                  pl.BlockSpec((LANES, LANES), lambda i, j: (0, 0))],
        out_specs=pl.BlockSpec((tm, tn), lambda i, j: (i, j)),
        out_shape=jax.ShapeDtypeStruct((m, n), BF16),
        scratch_shapes=[pltpu.VMEM((tm, d), BF16), pltpu.VMEM((tm, LANES), F32),
                        pltpu.VMEM((tm, LANES), F32), pltpu.VMEM((tm, LANES), F32)],
        compiler_params=_cparams(("parallel", "arbitrary")),
    )(x, g.reshape(1, d), w, pos, gains, freq, bd128)


def _attn_kernel(fast_ref, q_ref, k_ref, v_ref, lam_ref, subln_ref, o_ref, vt_ref, m_ref, ls_ref, l_ref,
                 acc_ref, *, tq, lambda_init):
    qi = pl.program_id(2)

    @pl.when(qi == 0)
    def _():
        for c in range(vt_ref.shape[0]):
            vt_ref[c] = v_ref[c * tq:(c + 1) * tq, :].astype(F32).T.astype(BF16)

    q = q_ref[...]
    lane = lax.broadcasted_iota(jnp.int32, q.shape, 1)
    zero = jnp.zeros_like(q)
    qs = jnp.concatenate([jnp.where(lane < HEAD, q, zero), jnp.where(lane < HEAD, zero, q)], axis=0)
    acc_ref[...] = jnp.zeros_like(acc_ref)

    def scores(j, masked):
        s = _nt(k_ref[pl.ds(pl.multiple_of(j * tq, tq), tq), :], qs)
        if masked:
            key = lax.broadcasted_iota(jnp.int32, (tq, tq), 0)
            qry = lax.broadcasted_iota(jnp.int32, (tq, tq), 1)
            keep = jnp.concatenate([key <= qry, key <= qry], axis=1)
            s = jnp.where(keep, s, NEG_BIG)
        return s

    def sweep(step):
        def body(j, carry):
            step(j, False)
            return carry
        lax.fori_loop(0, qi, body, 0)
        step(qi, True)

    @pl.when(fast_ref[0] == 1)
    def _():
        l_ref[...] = jnp.zeros_like(l_ref)

        def fold(p):
            return jnp.sum(p.reshape(tq // SUBLANES, SUBLANES, 2 * tq), axis=0)

        def step(j, masked):
            p = jnp.exp2(scores(j, masked))
            l_ref[...] += fold(p)
            acc_ref[...] += _dot(vt_ref[j], p.astype(BF16))

        def group(jj, carry):
            ps = [jnp.exp2(scores(KV_GROUP * jj + t, False)) for t in range(KV_GROUP)]
            l_ref[...] += sum(fold(p) for p in ps)
            acc_ref[...] += sum(_dot(vt_ref[KV_GROUP * jj + t], p.astype(BF16)) for t, p in enumerate(ps))
            return carry

        lax.fori_loop(0, qi // KV_GROUP, group, 0)

        def single(j, carry):
            step(j, False)
            return carry

        lax.fori_loop((qi // KV_GROUP) * KV_GROUP, qi, single, 0)
        step(qi, True)
        ls_ref[...] = jnp.sum(l_ref[...], axis=0, keepdims=True)

    @pl.when(fast_ref[0] == 0)
    def _():
        m_ref[...] = jnp.full_like(m_ref, NEG_BIG)
        ls_ref[...] = jnp.zeros_like(ls_ref)

        def step(j, masked):
            s = scores(j, masked)
            m_old = m_ref[...]
            m_new = jnp.maximum(m_old, jnp.max(s, axis=0, keepdims=True))
            alpha = jnp.exp2(m_old - m_new)
            p = jnp.exp2(s - m_new)
            ls_ref[...] = alpha * ls_ref[...] + jnp.sum(p, axis=0, keepdims=True)
            acc_ref[...] = alpha * acc_ref[...] + _dot(vt_ref[j], p.astype(BF16))
            m_ref[...] = m_new

        sweep(step)

    lq = lam_ref[...]
    lam = (jnp.exp(jnp.sum(lq[0:1] * lq[1:2], axis=-1, keepdims=True))
           - jnp.exp(jnp.sum(lq[2:3] * lq[3:4], axis=-1, keepdims=True)) + lambda_init)
    o = acc_ref[...] / ls_ref[...]
    o = o[:, :tq] - lam * o[:, tq:]
    ms = jnp.mean(o * o, axis=0, keepdims=True)
    o = o * lax.rsqrt(ms + RMS_EPS) * subln_ref[...] * (1.0 - lambda_init)
    o_ref[...] = o.T.astype(o_ref.dtype)


def diff_attention(qkv, fast, batch, seq, n_heads, lam_params, subln, lambda_init, *, tq=512):
    m = qkv.shape[0]
    tq = _tile(seq, tq)
    nq = seq // tq
    grid_spec = pltpu.PrefetchScalarGridSpec(
        num_scalar_prefetch=1,
        grid=(batch, n_heads, nq),
        in_specs=[pl.BlockSpec((tq, LANES), lambda b, h, i, f: (b * nq + i, h)),
                  pl.BlockSpec((seq, LANES), lambda b, h, i, f: (b, n_heads + h)),
                  pl.BlockSpec((seq, LANES), lambda b, h, i, f: (b, 2 * n_heads + h)),
                  pl.BlockSpec((4, HEAD), lambda b, h, i, f: (0, 0)),
                  pl.BlockSpec((LANES, 1), lambda b, h, i, f: (0, 0))],
        out_specs=pl.BlockSpec((tq, LANES), lambda b, h, i, f: (b * nq + i, h)),
        scratch_shapes=[pltpu.VMEM((nq, LANES, tq), BF16),
                        pltpu.VMEM((1, 2 * tq), F32), pltpu.VMEM((1, 2 * tq), F32),
                        pltpu.VMEM((SUBLANES, 2 * tq), F32), pltpu.VMEM((LANES, 2 * tq), F32)])
    return pl.pallas_call(
        functools.partial(_attn_kernel, tq=tq, lambda_init=lambda_init),
        grid_spec=grid_spec,
        out_shape=jax.ShapeDtypeStruct((m, n_heads * LANES), BF16),
        compiler_params=_cparams(("parallel", "parallel", "arbitrary")),
    )(fast, qkv, qkv, qkv, lam_params, subln.reshape(LANES, 1))


def _router_kernel(x_ref, g_ref, router_ref, ids_ref, gates_ref):
    h = _rms(x_ref[...], g_ref[...])
    logits = _dot(h, router_ref[...], precision=HIGHEST)
    lane = lax.broadcasted_iota(jnp.int32, logits.shape, 1)
    logits = jnp.where(lane < N_EXPERTS, logits, NEG_BIG)
    v1 = jnp.max(logits, axis=-1, keepdims=True)
    i1 = jnp.min(jnp.where(logits == v1, lane, LANES), axis=-1, keepdims=True)
    rest = jnp.where(lane == i1, NEG_BIG, logits)
    v2 = jnp.max(rest, axis=-1, keepdims=True)
    i2 = jnp.min(jnp.where(rest == v2, lane, LANES), axis=-1, keepdims=True)
    e2 = jnp.exp(v2 - v1)
    ids_ref[...] = jnp.where(lane == 0, i1, i2)
    gates_ref[...] = jnp.where(lane == 0, 1.0 / (1.0 + e2), e2 / (1.0 + e2))


def moe_router(x, g, router_p, *, tm=1024):
    m, d = x.shape
    tm = _tile(m, tm)
    return pl.pallas_call(
        _router_kernel,
        grid=(m // tm,),
        in_specs=[pl.BlockSpec((tm, d), lambda i: (i, 0)),
                  pl.BlockSpec((1, d), lambda i: (0, 0)),
                  pl.BlockSpec((d, LANES), lambda i: (0, 0))],
        out_specs=[pl.BlockSpec((tm, LANES), lambda i: (i, 0))] * 2,
        out_shape=[jax.ShapeDtypeStruct((m, LANES), jnp.int32), jax.ShapeDtypeStruct((m, LANES), F32)],
        compiler_params=_cparams(("parallel",)),
    )(x, g.reshape(1, d), router_p)


def _route_tables(ids, tm, n_tiles):
    n_pairs = ids.shape[0] * 2
    n_rows = n_tiles * tm
    shift = max(n_pairs, n_rows).bit_length()
    e_flat = ids.reshape(-1)
    experts = jnp.arange(N_EXPERTS, dtype=jnp.int32)
    counts = jnp.sum((e_flat[:, None] == experts[None, :]).astype(jnp.int32), axis=0)
    padded = ((counts + tm - 1) // tm) * tm
    ends = jnp.cumsum(padded)
    pad_ends = jnp.cumsum(padded - counts)
    q = jnp.arange(n_rows - n_pairs, dtype=jnp.int32)
    pad_expert = jnp.sum((q[:, None] >= pad_ends[None, :]).astype(jnp.int32), axis=1)
    low = (1 << shift) - 1
    keys = jnp.concatenate([(e_flat << shift) | jnp.arange(n_pairs, dtype=jnp.int32),
                            (pad_expert << shift) | low])
    keys = jnp.sort(keys)
    perm = jnp.where((keys & low) == low, -1, keys & low)
    starts = jnp.arange(n_tiles, dtype=jnp.int32) * tm
    tile_expert = jnp.sum((starts[:, None] >= ends[None, :]).astype(jnp.int32), axis=1)
    n_valid = ends[-1] // tm
    last_expert = jnp.sum(jnp.where(jnp.arange(n_tiles) == n_valid - 1, tile_expert, 0))
    tile_expert = jnp.where(jnp.arange(n_tiles) < n_valid, tile_expert, last_expert)
    n_real = jnp.sum((perm >= 0).reshape(n_tiles, tm).astype(jnp.int32), axis=1)
    return perm, tile_expert.astype(jnp.int32), n_valid.reshape(1).astype(jnp.int32), n_real


def _moe_group_kernel(te_ref, nv_ref, perm_ref, nr_ref, x_hbm, g_ref, wg_ref, wu_ref, wd_ref, y_hbm,
                      xbuf, ybuf, h_ref, acc_ref, gsem, ssem, *, tm, n_tokens):
    i = pl.program_id(0)
    f = pl.program_id(1)
    valid = i < nv_ref[0]
    base = i * tm
    d = h_ref.shape[1]
    n_chunks = d // LANES

    def tile_rows(k):
        return pl.ds(pl.multiple_of(k * SUBLANES, SUBLANES), SUBLANES)

    slot = i % 2

    def tile_in(s, r, t):
        return pltpu.make_async_copy(x_hbm.at[tile_rows(t), :], xbuf.at[s, tile_rows(r), :], gsem.at[s])

    def tile_out(r, dst):
        return pltpu.make_async_copy(ybuf.at[tile_rows(r), :], y_hbm.at[tile_rows(dst), :], ssem)

    def chunk_rows(c):
        return pl.ds(c, tm, stride=SUBLANES)

    def start_gather(tile, s):
        def start(r, carry):
            j = perm_ref[tile * tm + r]
            tile_in(s, r, jnp.maximum(j, 0) >> 1).start()
            return carry
        lax.fori_loop(0, tm, start, 0, unroll=8)

    def wait_scatter(count):
        def wait(r, carry):
            tile_out(r, 0).wait()
            return carry
        lax.fori_loop(0, count, wait, 0)

    @pl.when(valid & (f == 0))
    def _():
        @pl.when(i == 0)
        def _():
            start_gather(0, 0)

        def wait(r, carry):
            tile_in(slot, r, 0).wait()
            return carry
        lax.fori_loop(0, tm, wait, 0, unroll=8)

        @pl.when(i + 1 < nv_ref[0])
        def _():
            start_gather(i + 1, 1 - slot)

        ss = jnp.zeros((tm, 1), F32)
        for c in range(n_chunks):
            xc = xbuf[slot, chunk_rows(c), :]
            ss = ss + jnp.sum(xc * xc, axis=-1, keepdims=True)
        inv = lax.rsqrt(ss * (1.0 / d) + RMS_EPS)
        for c in range(n_chunks):
            sl = slice(c * LANES, (c + 1) * LANES)
            h_ref[:, sl] = (xbuf[slot, chunk_rows(c), :] * inv * g_ref[:, sl]).astype(BF16)
        acc_ref[...] = jnp.zeros_like(acc_ref)

    @pl.when(valid)
    def _():
        h = h_ref[...]
        gate = _dot(h, wg_ref[...])
        up = _dot(h, wu_ref[...])
        act = (gate * _sigmoid(gate) * up).astype(BF16)
        acc_ref[...] += _dot(act, wd_ref[...])

    @pl.when(valid & (f == pl.num_programs(1) - 1))
    def _():
        @pl.when(i > 0)
        def _():
            wait_scatter(nr_ref[jnp.maximum(i - 1, 0)])

        for c in range(n_chunks):
            ybuf[chunk_rows(c), :] = acc_ref[:, c * LANES:(c + 1) * LANES]

        n_real = nr_ref[i]

        def start(r):
            j = perm_ref[base + r]
            tile_out(r, (j & 1) * n_tokens + (j >> 1)).start()

        def start8(r8, carry):
            for k in range(SUBLANES):
                start(r8 * SUBLANES + k)
            return carry
        lax.fori_loop(0, n_real // SUBLANES, start8, 0)

        def start1(r, carry):
            start(r)
            return carry
        lax.fori_loop((n_real // SUBLANES) * SUBLANES, n_real, start1, 0)

        @pl.when(i == nv_ref[0] - 1)
        def _():
            wait_scatter(n_real)


def moe_experts(x8, g, perm, tile_expert, n_valid, n_real, wg, wu, wd, *, tm, tf=512):
    d = g.shape[0]
    m = x8.shape[0] // SUBLANES
    f = wg.shape[2]
    tf = _tile(f, tf)
    n_tiles, n_f = tile_expert.shape[0], f // tf
    fidx = lambda i, j, nv: jnp.where(i < nv[0], j, n_f - 1)
    grid_spec = pltpu.PrefetchScalarGridSpec(
        num_scalar_prefetch=4,
        grid=(n_tiles, n_f),
        in_specs=[pl.BlockSpec(memory_space=pl.ANY),
                  pl.BlockSpec((1, d), lambda i, j, te, nv, pm, nr: (0, 0)),
                  pl.BlockSpec((None, d, tf), lambda i, j, te, nv, pm, nr: (te[i], 0, fidx(i, j, nv))),
                  pl.BlockSpec((None, d, tf), lambda i, j, te, nv, pm, nr: (te[i], 0, fidx(i, j, nv))),
                  pl.BlockSpec((None, tf, d), lambda i, j, te, nv, pm, nr: (te[i], fidx(i, j, nv), 0))],
        out_specs=pl.BlockSpec(memory_space=pl.ANY),
        scratch_shapes=[pltpu.VMEM((2, tm * SUBLANES, LANES), F32), pltpu.VMEM((tm * SUBLANES, LANES), F32),
                        pltpu.VMEM((tm, d), BF16), pltpu.VMEM((tm, d), F32),
                        pltpu.SemaphoreType.DMA((2,)), pltpu.SemaphoreType.DMA(())])
    return pl.pallas_call(
        functools.partial(_moe_group_kernel, tm=tm, n_tokens=m),
        grid_spec=grid_spec,
        out_shape=jax.ShapeDtypeStruct((2 * m * SUBLANES, LANES), F32),
        compiler_params=_cparams(("arbitrary", "arbitrary")),
    )(tile_expert, n_valid, perm, n_real, x8, g.reshape(1, d), wg, wu, wd)


def _moe_combine_kernel(x_ref, y0_ref, y1_ref, gates_ref, o_ref, *, tm):
    gates = gates_ref[...]
    g0, g1 = gates[:, 0:1], gates[:, 1:2]
    for c in range(x_ref.shape[1] // LANES):
        sl = slice(c * LANES, (c + 1) * LANES)
        rows = pl.ds(c, tm, stride=SUBLANES)
        o_ref[:, sl] = x_ref[:, sl] + g0 * y0_ref[rows, :] + g1 * y1_ref[rows, :]


def moe_combine(x, y8, gates, *, tm=1024):
    m, d = x.shape
    tm = _tile(m, tm)
    nb = m // tm
    return pl.pallas_call(
        functools.partial(_moe_combine_kernel, tm=tm),
        grid=(nb,),
        in_specs=[pl.BlockSpec((tm, d), lambda i: (i, 0)),
                  pl.BlockSpec((tm * SUBLANES, LANES), lambda i: (i, 0)),
                  pl.BlockSpec((tm * SUBLANES, LANES), lambda i: (nb + i, 0)),
                  pl.BlockSpec((tm, LANES), lambda i: (i, 0))],
        out_specs=pl.BlockSpec((tm, d), lambda i: (i, 0)),
        out_shape=jax.ShapeDtypeStruct((m, d), F32),
        compiler_params=_cparams(("parallel",)),
    )(x, y8, y8, gates)


def moe(x, g, router_p, wg, wu, wd, *, tm=1024):
    m, d = x.shape
    assert d == SUBLANES * LANES
    tm = _tile(m, tm)
    n_tiles = (2 * m) // tm + N_EXPERTS
    ids, gates = moe_router(x, g, router_p)
    perm, tile_expert, n_valid, n_real = _route_tables(ids[:, :2], tm, n_tiles)
    x8 = x.reshape(m * SUBLANES, LANES)
    y8 = moe_experts(x8, g, perm, tile_expert, n_valid, n_real, wg, wu, wd, tm=tm)
    return moe_combine(x, y8, gates, tm=tm)


def _block_diag(blocks):
    n, h, _ = blocks.shape
    eye = jnp.eye(n, dtype=blocks.dtype)
    return (eye[:, None, :, None] * blocks[:, :, None, :]).reshape(n * h, n * h)


def _seg_ones(n):
    seg = jnp.arange(n) // HEAD
    return (seg[:, None] == seg[None, :]).astype(BF16)


def _even_layer(x, batch, seq, ln_mix, w_in, conv_w, conv_b, gate_a_w, gate_a_b, gate_x_w, gate_x_b,
                lru_lambda, shift_mu, w0, w2, a0, a2, g2, k_k, k_a, r_k, gn_w, gn_b, w_out,
                ln_ffn, ffn_gate, ffn_up, ffn_down):
    lru_w = lru_lambda.shape[0]
    rw_w = w0.shape[0]
    dl, al, gl = w2.shape[0], a2.shape[0], g2.shape[0]
    n_in = w_in.shape[1]
    proj = norm_matmul(x, ln_mix, w_in.astype(BF16), tn=n_in // 2)

    y_lru = lru_branch(proj, batch, seq, conv_w, conv_b,
                       _block_diag(gate_a_w).astype(BF16), gate_a_b,
                       _block_diag(gate_x_w).astype(BF16), gate_x_b, lru_lambda)

    lora = dl + al + gl
    zeros = lambda n: jnp.zeros((n, rw_w), F32)
    w2p = jnp.concatenate([w2, zeros(al + gl)], axis=0).astype(BF16)
    a2p = jnp.concatenate([zeros(dl), a2, zeros(gl)], axis=0).astype(BF16)
    g2p = jnp.concatenate([zeros(dl + al), g2], axis=0).astype(BF16)
    assert lora == w2p.shape[0]
    r, k, v, a, b, lw, g = rwkv_prep(proj, seq, 2 * lru_w, shift_mu, w0, w2p, a0, a2p, g2p, k_k, k_a,
                                     _seg_ones(rw_w))
    y_rwkv = rwkv_scan(r, k, v, a, b, lw, g, batch, seq, r_k.reshape(-1), gn_w, gn_b, _seg_ones(LANES))

    w_out = w_out.astype(BF16)
    x = matmul_residual([y_lru, y_rwkv], [w_out[:lru_w], w_out[lru_w:]], x)
    return ffn(x, ln_ffn, ffn_gate.astype(BF16), ffn_up.astype(BF16), ffn_down.astype(BF16))


def _odd_layer(x, pos, batch, seq, layer_idx, ln_mix, w_qkv, q_norm, k_norm, lq1, lk1, lq2, lk2, subln,
               w_o, ln_ffn, router, moe_gate, moe_up, moe_down):
    d = x.shape[1]
    n_heads = d // (2 * HEAD)
    qd = n_heads * 2 * HEAD
    lambda_init = 0.8 - 0.6 * math.exp(-0.3 * layer_idx)
    reps = qd // HEAD
    q_gain = q_norm * (HEAD ** -0.5 * LOG2E)
    logit_bound = 1.02 * HEAD * jnp.max(jnp.abs(q_gain)) * jnp.max(jnp.abs(k_norm))
    fast = (logit_bound <= MAX_EXP2_ARG).astype(jnp.int32).reshape(1)
    gains = jnp.concatenate([jnp.tile(q_gain, reps),
                             jnp.tile(k_norm, reps),
                             jnp.ones((w_qkv.shape[1] - 2 * qd,), F32)])[None, :]
    seg = jnp.arange(LANES) % HEAD
    inv_freq = ROPE_THETA ** (-(2.0 * (seg % (ROPE_DIM // 2))).astype(F32) / ROPE_DIM)
    freq = jnp.where(seg < ROPE_DIM, inv_freq, 0.0)[None, :].astype(F32)
    qkv = qkv_project(x, ln_mix, w_qkv.astype(BF16), pos, gains, freq, _seg_ones(LANES), 2 * qd)
    lam_params = jnp.stack([lq1, lk1, lq2, lk2]).astype(F32)
    o = diff_attention(qkv, fast, batch, seq, n_heads, lam_params, subln, lambda_init)
    x = matmul_residual([o], [w_o.astype(BF16)], x)
    router_p = jnp.pad(router, ((0, 0), (0, LANES - router.shape[1])))
    return moe(x, ln_ffn, router_p, moe_gate.astype(BF16), moe_up.astype(BF16), moe_down.astype(BF16))


def kernel(x, positions, e_ln_mix, e_w_in, e_conv_w, e_conv_b, e_gate_a_w, e_gate_a_b, e_gate_x_w, e_gate_x_b, e_lru_lambda, e_shift_mu, e_w0, e_w2, e_a0, e_a2, e_g2, e_k_k, e_k_a, e_r_k, e_gn_w, e_gn_b, e_w_out, e_ln_ffn, e_ffn_gate, e_ffn_up, e_ffn_down, o_ln_mix, o_w_qkv, o_q_norm, o_k_norm, o_lambda_q1, o_lambda_k1, o_lambda_q2, o_lambda_k2, o_subln, o_w_o, o_ln_ffn, o_router, o_moe_gate, o_moe_up, o_moe_down):
    batch, seq, d = x.shape
    depth = e_ln_mix.shape[0] + o_ln_mix.shape[0]
    xf = x.reshape(batch * seq, d)
    pos = positions.reshape(batch * seq, 1).astype(F32)
    for i in range(depth):
        j = i // 2
        if i % 2 == 0:
            xf = _even_layer(xf, batch, seq, e_ln_mix[j], e_w_in[j], e_conv_w[j], e_conv_b[j],
                             e_gate_a_w[j], e_gate_a_b[j], e_gate_x_w[j], e_gate_x_b[j],
                             e_lru_lambda[j], e_shift_mu[j], e_w0[j], e_w2[j], e_a0[j], e_a2[j],
                             e_g2[j], e_k_k[j], e_k_a[j], e_r_k[j], e_gn_w[j], e_gn_b[j], e_w_out[j],
                             e_ln_ffn[j], e_ffn_gate[j], e_ffn_up[j], e_ffn_down[j])
        else:
            xf = _odd_layer(xf, pos, batch, seq, i, o_ln_mix[j], o_w_qkv[j], o_q_norm[j], o_k_norm[j],
                            o_lambda_q1[j], o_lambda_k1[j], o_lambda_q2[j], o_lambda_k2[j], o_subln[j],
                            o_w_o[j], o_ln_ffn[j], o_router[j], o_moe_gate[j], o_moe_up[j],
                            o_moe_down[j])
    return xf.reshape(batch, seq, d)
```
